```python
import jax, jax.numpy as jnp
from jax import lax
import numpy as np

D_MODEL = 4096
BATCH = 1
SEQ = 8192
DEPTH = 2

N_BRANCH = 4
BRANCH_W = D_MODEL // 4
MLSTM_HEADS = 8
MLSTM_DH = BRANCH_W // MLSTM_HEADS
MLSTM_CHUNK = 128
GMLP_GROUPS = 8
GMLP_GW = BRANCH_W // GMLP_GROUPS
GMLP_CHUNK = 128
RGLRU_BLOCKS = 8
RGLRU_BW = BRANCH_W // RGLRU_BLOCKS
RGLRU_CONV = 4
RGLRU_C = 8.0
SCONV_K = 3
EPS = 1e-6

SPLIT_SIZES = (
    5 * BRANCH_W + 2 * MLSTM_HEADS,
    3 * BRANCH_W,
    2 * BRANCH_W,
    4 * BRANCH_W,
    N_BRANCH * D_MODEL,
)
SPLIT_POINTS = tuple(int(p) for p in np.cumsum(SPLIT_SIZES)[:-1])
PROJ_W = sum(SPLIT_SIZES)

kernel_name = 'hybrid_gated_parallel_mixers'


def rms_norm(x, w):
    xf = x.astype(jnp.float32)
    y = xf * lax.rsqrt(jnp.mean(xf * xf, axis=-1, keepdims=True) + EPS)
    return (y * w.astype(jnp.float32)).astype(x.dtype)


def layer_norm(x, w, b):
    xf = x.astype(jnp.float32)
    mu = jnp.mean(xf, axis=-1, keepdims=True)
    xc = xf - mu
    var = jnp.mean(xc * xc, axis=-1, keepdims=True)
    return (xc * lax.rsqrt(var + EPS) * w.astype(jnp.float32) + b.astype(jnp.float32)).astype(x.dtype)


def causal_dwconv(x, w):
    K = w.shape[0]
    S = x.shape[1]
    xp = jnp.pad(x, ((0, 0), (K - 1, 0), (0, 0)))
    y = xp[:, 0:S] * w[0]
    for k in range(1, K):
        y = y + xp[:, k:k + S] * w[k]
    return y


def mlstm_chunkwise(q, k, v, i_pre, f_pre):
    B, S, NH, DH = q.shape
    L = MLSTM_CHUNK
    NC = S // L
    k = k * (DH ** -0.5)
    lf = jax.nn.log_sigmoid(f_pre)

    def to_chunks(t):
        return t.reshape(B, NC, L, NH, DH).transpose(1, 0, 3, 2, 4)

    def gate_chunks(t):
        return t.reshape(B, NC, L, NH).transpose(1, 0, 3, 2)

    tril = jnp.tril(jnp.ones((L, L), dtype=bool))

    def step(carry, xs):
        C, n, m = carry
        qc, kc, vc, ic, fc = xs
        b = jnp.cumsum(fc, axis=-1)
        inter = b + m[..., None]
        dmat = jnp.where(tril, b[..., :, None] - b[..., None, :] + ic[..., None, :], -jnp.inf)
        m_t = jnp.maximum(inter, jnp.max(dmat, axis=-1))
        w_intra = jnp.exp(dmat - m_t[..., None])
        w_inter = jnp.exp(inter - m_t)
        scores = jnp.einsum('bhtd,bhsd->bhts', qc, kc) * w_intra
        num = jnp.einsum('bhts,bhse->bhte', scores, vc) + w_inter[..., None] * jnp.einsum('bhed,bhtd->bhte', C, qc)
        den = jnp.sum(scores, axis=-1) + w_inter * jnp.einsum('bhd,bhtd->bht', n, qc)
        h = num / jnp.maximum(jnp.abs(den), jnp.exp(-m_t))[..., None]
        b_last = b[..., -1]
        w_src = b_last[..., None] - b + ic
        m_new = jnp.maximum(b_last + m, jnp.max(w_src, axis=-1))
        decay = jnp.exp(b_last + m - m_new)
        ws = jnp.exp(w_src - m_new[..., None])
        C_new = decay[..., None, None] * C + jnp.einsum('bhs,bhse,bhsd->bhed', ws, vc, kc)
        n_new = decay[..., None] * n + jnp.einsum('bhs,bhsd->bhd', ws, kc)
        return (C_new, n_new, m_new), h

    init = (jnp.zeros((B, NH, DH, DH), jnp.float32),
            jnp.zeros((B, NH, DH), jnp.float32),
            jnp.zeros((B, NH), jnp.float32))
    xs = (to_chunks(q), to_chunks(k), to_chunks(v), gate_chunks(i_pre), gate_chunks(lf))
    _, h = lax.scan(step, init, xs)
    return h.transpose(1, 0, 3, 2, 4).reshape(B, S, NH * DH)


def rglru_scan(x, r_pre, i_pre, a_param):
    xf = x.astype(jnp.float32)
    r = jax.nn.sigmoid(r_pre.astype(jnp.float32))
    ig = jax.nn.sigmoid(i_pre.astype(jnp.float32))
    log_a = -RGLRU_C * r * jax.nn.softplus(-a_param.astype(jnp.float32))
    a = jnp.exp(log_a)
    mult = jnp.sqrt(-jnp.expm1(2.0 * log_a))
    first = (jnp.arange(x.shape[1]) == 0)[None, :, None]
    mult = jnp.where(first, 1.0, mult)
    bx = mult * ig * xf

    def combine(lhs, rhs):
        a1, b1 = lhs
        a2, b2 = rhs
        return a1 * a2, a2 * b1 + b2

    _, h = lax.associative_scan(combine, (a, bx), axis=1)
    return h.astype(x.dtype)


def hybrid_layer(x, pre_w, post_w, w_in, mlstm_gate_bias, mlstm_norm_w, gmlp_ln_w, gmlp_ln_b,
                 gmlp_w_s, gmlp_b_s, rglru_conv_w, rglru_conv_b, rglru_w_gate, rglru_b_gate,
                 rglru_a_param, sconv_w, w_branch, w_out):
    B, S, _ = x.shape
    h = rms_norm(x, pre_w)
    proj = h @ w_in
    p_a, p_b, p_c, p_d, p_g = jnp.split(proj, SPLIT_POINTS, axis=-1)

    q, k, v, o, z_a = jnp.split(p_a[..., :5 * BRANCH_W], 5, axis=-1)
    gates = p_a[..., 5 * BRANCH_W:].astype(jnp.float32) + mlstm_gate_bias.astype(jnp.float32)
    i_pre, f_pre = gates[..., :MLSTM_HEADS], gates[..., MLSTM_HEADS:]
    heads = lambda t: t.astype(jnp.float32).reshape(B, S, MLSTM_HEADS, MLSTM_DH)
    y_a = mlstm_chunkwise(heads(q), heads(k), heads(v), i_pre, f_pre)
    y_a = rms_norm(y_a.reshape(B, S, MLSTM_HEADS, MLSTM_DH),
                   mlstm_norm_w.reshape(MLSTM_HEADS, MLSTM_DH)).reshape(B, S, BRANCH_W)
    y_a = (y_a * jax.nn.sigmoid(o.astype(jnp.float32))).astype(x.dtype) * jax.nn.silu(z_a)

    u, vg, z_b = jnp.split(p_b, 3, axis=-1)
    vg = layer_norm(vg, gmlp_ln_w, gmlp_ln_b).reshape(B, S // GMLP_CHUNK, GMLP_CHUNK, GMLP_GROUPS, GMLP_GW)
    w_causal = jnp.tril(gmlp_w_s)
    sp = jnp.einsum('gts,bnsgc->bntgc', w_causal, vg) + gmlp_b_s.T[None, None, :, :, None]
    y_b = u * sp.reshape(B, S, BRANCH_W) * jax.nn.silu(z_b)

    xc, z_c = jnp.split(p_c, 2, axis=-1)
    xc = causal_dwconv(xc, rglru_conv_w) + rglru_conv_b
    gt = jnp.einsum('bsnc,ncd->bsnd', xc.reshape(B, S, RGLRU_BLOCKS, RGLRU_BW), rglru_w_gate)
    r_pre = gt[..., :RGLRU_BW].reshape(B, S, BRANCH_W) + rglru_b_gate[:BRANCH_W]
    ri_pre = gt[..., RGLRU_BW:].reshape(B, S, BRANCH_W) + rglru_b_gate[BRANCH_W:]
    y_c = rglru_scan(xc, r_pre, ri_pre, rglru_a_param) * jax.nn.silu(z_c)

    bg, cg, xd, z_d = jnp.split(p_d, 4, axis=-1)
    y_d = bg * causal_dwconv(cg * xd, sconv_w) * jax.nn.silu(z_d)

    ys = jnp.stack([y_a, y_b, y_c, y_d], axis=2)
    branch = jnp.einsum('bsgc,gcd->bsgd', ys, w_branch)
    gate = jax.nn.sigmoid(p_g.reshape(B, S, N_BRANCH, D_MODEL))
    merged = jnp.einsum('bsgd,bsgd->bsd', gate, branch)
    out = merged @ w_out
    return x + rms_norm(out, post_w)


def setup_inputs(seed: int = 0) -> dict:
    key = jax.random.key(seed)
    ks = jax.random.split(key, 20)
    f32 = jnp.float32
    nrm = lambda k, shape, s: jax.random.normal(k, shape, f32) * s
    x = jax.random.normal(ks[0], (BATCH, SEQ, D_MODEL), f32)
    pre_w = 1.0 + nrm(ks[1], (DEPTH, D_MODEL), 0.1)
    post_w = 1.0 + nrm(ks[2], (DEPTH, D_MODEL), 0.1)
    w_in = nrm(ks[3], (DEPTH, D_MODEL, PROJ_W), D_MODEL ** -0.5)
    f_bias = jnp.linspace(3.0, 6.0, MLSTM_HEADS, dtype=f32)
    mlstm_gate_bias = jnp.concatenate([
        nrm(ks[4], (DEPTH, MLSTM_HEADS), 0.1),
        f_bias[None, :] + nrm(ks[5], (DEPTH, MLSTM_HEADS), 0.1)], axis=-1)
    mlstm_norm_w = 1.0 + nrm(ks[6], (DEPTH, BRANCH_W), 0.1)
    gmlp_ln_w = 1.0 + nrm(ks[7], (DEPTH, BRANCH_W), 0.1)
    gmlp_ln_b = nrm(ks[8], (DEPTH, BRANCH_W), 0.01)
    gmlp_w_s = nrm(ks[9], (DEPTH, GMLP_GROUPS, GMLP_CHUNK, GMLP_CHUNK), GMLP_CHUNK ** -0.5)
    gmlp_b_s = 1.0 + nrm(ks[10], (DEPTH, GMLP_GROUPS, GMLP_CHUNK), 0.1)
    rglru_conv_w = nrm(ks[11], (DEPTH, RGLRU_CONV, BRANCH_W), RGLRU_CONV ** -0.5)
    rglru_conv_b = nrm(ks[12], (DEPTH, BRANCH_W), 0.01)
    rglru_w_gate = nrm(ks[13], (DEPTH, RGLRU_BLOCKS, RGLRU_BW, 2 * RGLRU_BW), RGLRU_BW ** -0.5)
    rglru_b_gate = nrm(ks[14], (DEPTH, 2 * BRANCH_W), 0.01)
    u = jax.random.uniform(ks[15], (DEPTH, BRANCH_W), f32, 0.9, 0.999)
    s = u ** (1.0 / RGLRU_C)
    rglru_a_param = jnp.log(s) - jnp.log1p(-s)
    sconv_w = nrm(ks[16], (DEPTH, SCONV_K, BRANCH_W), SCONV_K ** -0.5)
    w_branch = nrm(ks[17], (DEPTH, N_BRANCH, BRANCH_W, D_MODEL), BRANCH_W ** -0.5)
    w_out = nrm(ks[18], (DEPTH, D_MODEL, D_MODEL), D_MODEL ** -0.5)
    return {'x': x, 'pre_w': pre_w, 'post_w': post_w, 'w_in': w_in,
            'mlstm_gate_bias': mlstm_gate_bias, 'mlstm_norm_w': mlstm_norm_w,
            'gmlp_ln_w': gmlp_ln_w, 'gmlp_ln_b': gmlp_ln_b, 'gmlp_w_s': gmlp_w_s, 'gmlp_b_s': gmlp_b_s,
            'rglru_conv_w': rglru_conv_w, 'rglru_conv_b': rglru_conv_b, 'rglru_w_gate': rglru_w_gate,
            'rglru_b_gate': rglru_b_gate, 'rglru_a_param': rglru_a_param, 'sconv_w': sconv_w,
            'w_branch': w_branch, 'w_out': w_out}


def reference(x, pre_w, post_w, w_in, mlstm_gate_bias, mlstm_norm_w, gmlp_ln_w, gmlp_ln_b,
              gmlp_w_s, gmlp_b_s, rglru_conv_w, rglru_conv_b, rglru_w_gate, rglru_b_gate,
              rglru_a_param, sconv_w, w_branch, w_out):
    for l in range(DEPTH):
        x = hybrid_layer(x, pre_w[l], post_w[l], w_in[l], mlstm_gate_bias[l], mlstm_norm_w[l],
                         gmlp_ln_w[l], gmlp_ln_b[l], gmlp_w_s[l], gmlp_b_s[l], rglru_conv_w[l],
                         rglru_conv_b[l], rglru_w_gate[l], rglru_b_gate[l], rglru_a_param[l],
                         sconv_w[l], w_branch[l], w_out[l])
    return x
```

```python
import functools

import jax
import jax.numpy as jnp
from jax import lax
from jax.experimental import pallas as pl
from jax.experimental.pallas import tpu as pltpu

F32 = jnp.float32
BF16 = jnp.bfloat16

D_MODEL = 4096
BRANCH_W = D_MODEL // 4
N_BRANCH = 4
HEADS = 8
HEAD_DIM = BRANCH_W // HEADS
CHUNK = 128
GROUPS = 8
GROUP_W = BRANCH_W // GROUPS
LRU_BLOCKS = 8
LRU_BW = BRANCH_W // LRU_BLOCKS
LRU_C = 8.0
EPS = 1e-6

LANES = 128
SUBLANES = 8
MIX_W = 14 * BRANCH_W
GATE_COL0 = 5 * BRANCH_W
MIX_COL1 = GATE_COL0 + 2 * HEADS
MERGE_COL0 = MIX_COL1 + 9 * BRANCH_W

TM_NORM = 256
TM_MM = 1024
TN_MM = 1024
TM_MERGE = 1024
TN_MERGE = 512
TM_MLSTM = CHUNK
TM_GMLP = 256
TM_LRU = 256
TM_CONV = 256
VMEM_MM = 52 * 1024 * 1024
VMEM_SMALL = 40 * 1024 * 1024


def _params(semantics, vmem):
    return pltpu.CompilerParams(dimension_semantics=semantics, vmem_limit_bytes=vmem)


def _sigmoid(x):
    return 1.0 / (1.0 + jnp.exp(-x))


def _silu(x):
    return x * _sigmoid(x)


def _softplus(x):
    return jnp.maximum(x, 0.0) + jnp.log1p(jnp.exp(-jnp.abs(x)))


def _expm1(y):
    u = jnp.exp(y)
    um1 = u - 1.0
    lg = jnp.log(u)
    r = um1 * y / jnp.where(lg == 0.0, 1.0, lg)
    return jnp.where(um1 == 0.0, y, jnp.where(um1 == -1.0, -1.0, r))


def _row_iota(shape):
    return lax.broadcasted_iota(jnp.int32, shape, 0)


def _shift_rows(x, d, fill):
    rows, cols = x.shape
    if d % SUBLANES == 0:
        return jnp.concatenate([jnp.full((d, cols), fill, x.dtype), x[:rows - d]], axis=0)
    return jnp.where(_row_iota(x.shape) >= d, pltpu.roll(x, d, 0), fill)


def _causal_conv(x, prev, w):
    taps = w.shape[0]

    def conv(arr):
        y = pltpu.roll(arr, taps - 1, 0) * w[0:1]
        for k in range(1, taps - 1):
            y = y + pltpu.roll(arr, taps - 1 - k, 0) * w[k:k + 1]
        return y + arr * w[taps - 1:taps]

    body = conv(x)
    head = conv(jnp.concatenate([prev, x[:SUBLANES]], axis=0))[SUBLANES:]
    return jnp.concatenate([head, body[SUBLANES:]], axis=0)


def _rms(x):
    return x * lax.rsqrt(jnp.mean(x * x, axis=-1, keepdims=True) + EPS)


def _norm_proj(x, pw_ref, wi_ref, wf_ref, h_ref, gi_ref, gf_ref):
    hb = (_rms(x) * pw_ref[...]).astype(BF16)
    h_ref[...] = hb
    gi_ref[...] = jnp.dot(hb, wi_ref[...], preferred_element_type=F32)
    gf_ref[...] = jnp.dot(hb, wf_ref[...], preferred_element_type=F32)


def _prenorm_body(x_ref, pw_ref, wi_ref, wf_ref, h_ref, gi_ref, gf_ref):
    _norm_proj(x_ref[...], pw_ref, wi_ref, wf_ref, h_ref, gi_ref, gf_ref)


def _post_body(x_ref, o_ref, qw_ref, xo_ref):
    xo_ref[...] = x_ref[...] + _rms(o_ref[...]) * qw_ref[...]


def _post_next_body(x_ref, o_ref, qw_ref, pw_ref, wi_ref, wf_ref, xo_ref, h_ref, gi_ref, gf_ref):
    xn = x_ref[...] + _rms(o_ref[...]) * qw_ref[...]
    xo_ref[...] = xn
    _norm_proj(xn, pw_ref, wi_ref, wf_ref, h_ref, gi_ref, gf_ref)


def _row_spec(tm, width):
    return pl.BlockSpec((tm, width), lambda i: (i, 0))


def _full_spec(shape):
    return pl.BlockSpec(shape, lambda *_: (0,) * len(shape))


def _norm_out(seq):
    shapes = (jax.ShapeDtypeStruct((seq, D_MODEL), BF16),
              jax.ShapeDtypeStruct((seq, LANES), F32),
              jax.ShapeDtypeStruct((seq, LANES), F32))
    specs = (_row_spec(TM_NORM, D_MODEL), _row_spec(TM_NORM, LANES), _row_spec(TM_NORM, LANES))
    return shapes, specs


def _prenorm(x, pw, wi, wf):
    seq = x.shape[0]
    shapes, specs = _norm_out(seq)
    return pl.pallas_call(
        _prenorm_body,
        grid=(seq // TM_NORM,),
        in_specs=[_row_spec(TM_NORM, D_MODEL), _full_spec((1, D_MODEL)),
                  _full_spec((D_MODEL, LANES)), _full_spec((D_MODEL, LANES))],
        out_specs=specs, out_shape=shapes,
        compiler_params=_params(("parallel",), VMEM_SMALL),
        name="prenorm",
    )(x, pw, wi, wf)


def _post(x, o, qw):
    seq = x.shape[0]
    return pl.pallas_call(
        _post_body,
        grid=(seq // TM_NORM,),
        in_specs=[_row_spec(TM_NORM, D_MODEL), _row_spec(TM_NORM, D_MODEL), _full_spec((1, D_MODEL))],
        out_specs=_row_spec(TM_NORM, D_MODEL),
        out_shape=jax.ShapeDtypeStruct((seq, D_MODEL), F32),
        compiler_params=_params(("parallel",), VMEM_SMALL),
        name="post",
    )(x, o, qw)


def _post_next(x, o, qw, pw, wi, wf):
    seq = x.shape[0]
    shapes, specs = _norm_out(seq)
    return pl.pallas_call(
        _post_next_body,
        grid=(seq // TM_NORM,),
        in_specs=[_row_spec(TM_NORM, D_MODEL), _row_spec(TM_NORM, D_MODEL), _full_spec((1, D_MODEL)),
                  _full_spec((1, D_MODEL)), _full_spec((D_MODEL, LANES)), _full_spec((D_MODEL, LANES))],
        out_specs=(_row_spec(TM_NORM, D_MODEL),) + specs,
        out_shape=(jax.ShapeDtypeStruct((seq, D_MODEL), F32),) + shapes,
        compiler_params=_params(("parallel",), VMEM_SMALL),
        name="post_next",
    )(x, o, qw, pw, wi, wf)


def _mm_body(a_ref, b_ref, o_ref):
    o_ref[...] = jnp.dot(a_ref[...], b_ref[...], preferred_element_type=F32).astype(o_ref.dtype)


def _matmul(a, b, out_dtype, name):
    m, k = a.shape
    n = b.shape[1]
    return pl.pallas_call(
        _mm_body,
        grid=(m // TM_MM, n // TN_MM),
        in_specs=[pl.BlockSpec((TM_MM, k), lambda i, j: (i, 0)),
                  pl.BlockSpec((k, TN_MM), lambda i, j: (0, j))],
        out_specs=pl.BlockSpec((TM_MM, TN_MM), lambda i, j: (i, j)),
        out_shape=jax.ShapeDtypeStruct((m, n), out_dtype),
        compiler_params=_params(("parallel", "parallel"), VMEM_MM),
        name=name,
    )(a, b)


def _mlstm_body(q_ref, k_ref, v_ref, o_ref, z_ref, gi_ref, gf_ref, bi_ref, bf_ref, nw_ref,
                y_ref, state_ref, m_ref):
    @pl.when(pl.program_id(0) == 0)
    def _():
        state_ref[...] = jnp.zeros_like(state_ref)
        m_ref[...] = jnp.zeros_like(m_ref)

    i_pre = gi_ref[...] + bi_ref[...]
    f_pre = gf_ref[...] + bf_ref[...]
    lf = -_softplus(-f_pre)
    b = lf
    d = 1
    while d < CHUNK:
        b = b + _shift_rows(b, d, 0.0)
        d *= 2
    beta = i_pre - b
    cm = beta
    d = 1
    while d < CHUNK:
        cm = jnp.maximum(cm, _shift_rows(cm, d, -jnp.inf))
        d *= 2
    m_prev = m_ref[...]
    inter = b + m_prev
    m_t = jnp.maximum(inter, b + cm)
    alpha = b - m_t
    w_inter = jnp.exp(inter - m_t)
    e_negm = jnp.exp(-m_t)
    b_last = b[CHUNK - 1:CHUNK]
    m_new = jnp.maximum(b_last + m_prev, b_last + cm[CHUNK - 1:CHUNK])
    decay = jnp.exp(b_last + m_prev - m_new)
    ws = jnp.exp(b_last + beta - m_new)
    m_ref[...] = m_new
    beta_t = beta.T

    shape = (CHUNK, CHUNK)
    tril = lax.broadcasted_iota(jnp.int32, shape, 0) >= lax.broadcasted_iota(jnp.int32, shape, 1)
    lane0 = lax.broadcasted_iota(jnp.int32, shape, 1) == 0

    for h in range(HEADS):
        sl = slice(h * HEAD_DIM, (h + 1) * HEAD_DIM)
        q = q_ref[:, sl].astype(BF16)
        k_t = (k_ref[:, sl] * (HEAD_DIM ** -0.5)).T.astype(BF16)
        v = v_ref[:, sl]
        qk = jnp.dot(q, k_t, preferred_element_type=F32)
        dmat = alpha[:, h:h + 1] + beta_t[h:h + 1, :]
        scores = qk * jnp.exp(jnp.where(tril, dmat, -jnp.inf))
        state = state_ref[h]
        q_state = jnp.dot(q, state.astype(BF16), preferred_element_type=F32)
        wi_col = w_inter[:, h:h + 1]
        num = (jnp.dot(scores.astype(BF16), v.astype(BF16), preferred_element_type=F32)
               + wi_col * q_state[:, :HEAD_DIM])
        den = jnp.sum(scores, axis=-1, keepdims=True) + wi_col * q_state[:, HEAD_DIM:HEAD_DIM + 1]
        hh = num / jnp.maximum(jnp.abs(den), e_negm[:, h:h + 1])
        y = _rms(hh) * nw_ref[:, sl]
        y = (y * _sigmoid(o_ref[:, sl])) * _silu(z_ref[:, sl])
        y_ref[:, sl] = y.astype(y_ref.dtype)
        ws_col = ws[:, h:h + 1]
        upd = jnp.concatenate([(ws_col * v).astype(BF16),
                               jnp.where(lane0, ws_col, 0.0).astype(BF16)], axis=1)
        state_ref[h] = decay[:, h:h + 1] * state + jnp.dot(k_t, upd, preferred_element_type=F32)


def _mix_spec(tm, col_block):
    return pl.BlockSpec((tm, BRANCH_W), lambda i: (i, col_block))


def _mlstm(p_mix, gi, gf, bias_i, bias_f, norm_w):
    seq = p_mix.shape[0]
    tm = TM_MLSTM
    return pl.pallas_call(
        _mlstm_body,
        grid=(seq // tm,),
        in_specs=[_mix_spec(tm, 0), _mix_spec(tm, 1), _mix_spec(tm, 2), _mix_spec(tm, 3), _mix_spec(tm, 4),
                  _row_spec(tm, LANES), _row_spec(tm, LANES),
                  _full_spec((1, LANES)), _full_spec((1, LANES)), _full_spec((1, BRANCH_W))],
        out_specs=_row_spec(tm, BRANCH_W),
        out_shape=jax.ShapeDtypeStruct((seq, BRANCH_W), BF16),
        scratch_shapes=[pltpu.VMEM((HEADS, HEAD_DIM, 2 * HEAD_DIM), F32), pltpu.VMEM((1, LANES), F32)],
        compiler_params=_params(("arbitrary",), VMEM_SMALL),
        name="mlstm",
    )(p_mix, p_mix, p_mix, p_mix, p_mix, gi, gf, bias_i, bias_f, norm_w)


def _gmlp_body(u_ref, v_ref, z_ref, lw_ref, lb_ref, ws_ref, bst_ref, y_ref):
    v = v_ref[...]
    vc = v - jnp.mean(v, axis=-1, keepdims=True)
    var = jnp.mean(vc * vc, axis=-1, keepdims=True)
    vn = (vc * lax.rsqrt(var + EPS) * lw_ref[...] + lb_ref[...]).astype(BF16)
    shape = (CHUNK, CHUNK)
    tril = lax.broadcasted_iota(jnp.int32, shape, 0) >= lax.broadcasted_iota(jnp.int32, shape, 1)
    for g in range(GROUPS):
        cols = slice(g * GROUP_W, (g + 1) * GROUP_W)
        w = jnp.where(tril, ws_ref[g], 0.0).astype(BF16)
        bias = bst_ref[:, g:g + 1]
        for c in range(v.shape[0] // CHUNK):
            rows = slice(c * CHUNK, (c + 1) * CHUNK)
            sp = jnp.dot(w, vn[rows, cols], preferred_element_type=F32) + bias
            y = u_ref[rows, cols] * sp * _silu(z_ref[rows, cols])
            y_ref[rows, cols] = y.astype(y_ref.dtype)


def _gmlp(p_mix, ln_w, ln_b, w_s, b_s_t):
    seq = p_mix.shape[0]
    tm = TM_GMLP
    return pl.pallas_call(
        _gmlp_body,
        grid=(seq // tm,),
        in_specs=[_mix_spec(tm, 5), _mix_spec(tm, 6), _mix_spec(tm, 7),
                  _full_spec((1, BRANCH_W)), _full_spec((1, BRANCH_W)),
                  _full_spec((GROUPS, CHUNK, CHUNK)), _full_spec((CHUNK, GROUPS))],
        out_specs=_row_spec(tm, BRANCH_W),
        out_shape=jax.ShapeDtypeStruct((seq, BRANCH_W), BF16),
        compiler_params=_params(("parallel",), VMEM_SMALL),
        name="gmlp",
    )(p_mix, p_mix, p_mix, ln_w, ln_b, w_s, b_s_t)


def _linear_scan(a, b):
    d = 1
    while d < a.shape[0]:
        b = b + a * _shift_rows(b, d, 0.0)
        a = a * _shift_rows(a, d, 1.0)
        d *= 2
    return a, b


def _rglru_body(x_ref, z_ref, cw_ref, cb_ref, wg_ref, bg_ref, ap_ref, y_ref, prev_ref, carry_ref):
    step = pl.program_id(0)

    @pl.when(step == 0)
    def _():
        prev_ref[...] = jnp.zeros_like(prev_ref)
        carry_ref[...] = jnp.zeros_like(carry_ref)

    x = x_ref[...]
    tm = x.shape[0]
    xc = _causal_conv(x, prev_ref[...], cw_ref[...]) + cb_ref[...]
    prev_ref[...] = x[tm - SUBLANES:]
    sp = _softplus(-ap_ref[...])
    seq_start = (_row_iota((tm, LRU_BW)) == 0) & (step == 0)
    for n in range(LRU_BLOCKS):
        cols = slice(n * LRU_BW, (n + 1) * LRU_BW)
        xn = xc[:, cols]
        gt = jnp.dot(xn.astype(BF16), wg_ref[n].astype(BF16), preferred_element_type=F32)
        r = _sigmoid(gt[:, :LRU_BW] + bg_ref[:, cols])
        ig = _sigmoid(gt[:, LRU_BW:] + bg_ref[:, BRANCH_W + n * LRU_BW:BRANCH_W + (n + 1) * LRU_BW])
        log_a = -LRU_C * r * sp[:, cols]
        a = jnp.exp(log_a)
        mult = jnp.sqrt(-_expm1(2.0 * log_a))
        mult = jnp.where(seq_start, 1.0, mult)
        a_cum, hs = _linear_scan(a, mult * ig * xn)
        hs = hs + a_cum * carry_ref[:, cols]
        carry_ref[:, cols] = hs[tm - 1:]
        y_ref[:, cols] = (hs * _silu(z_ref[:, cols])).astype(y_ref.dtype)


def _rglru(p_mix, conv_w, conv_b, w_gate, b_gate, a_param):
    seq = p_mix.shape[0]
    tm = TM_LRU
    return pl.pallas_call(
        _rglru_body,
        grid=(seq // tm,),
        in_specs=[_mix_spec(tm, 8), _mix_spec(tm, 9),
                  _full_spec(conv_w.shape), _full_spec((1, BRANCH_W)),
                  _full_spec(w_gate.shape), _full_spec((1, 2 * BRANCH_W)), _full_spec((1, BRANCH_W))],
        out_specs=_row_spec(tm, BRANCH_W),
        out_shape=jax.ShapeDtypeStruct((seq, BRANCH_W), BF16),
        scratch_shapes=[pltpu.VMEM((SUBLANES, BRANCH_W), F32), pltpu.VMEM((1, BRANCH_W), F32)],
        compiler_params=_params(("arbitrary",), VMEM_SMALL),
        name="rglru",
    )(p_mix, p_mix, conv_w, conv_b, w_gate, b_gate, a_param)


def _sconv_body(b_ref, c_ref, x_ref, z_ref, w_ref, y_ref, prev_ref):
    @pl.when(pl.program_id(0) == 0)
    def _():
        prev_ref[...] = jnp.zeros_like(prev_ref)

    p = c_ref[...] * x_ref[...]
    conv = _causal_conv(p, prev_ref[...], w_ref[...])
    prev_ref[...] = p[p.shape[0] - SUBLANES:]
    y_ref[...] = (b_ref[...] * conv * _silu(z_ref[...])).astype(y_ref.dtype)


def _sconv(p_mix, w):
    seq = p_mix.shape[0]
    tm = TM_CONV
    return pl.pallas_call(
        _sconv_body,
        grid=(seq // tm,),
        in_specs=[_mix_spec(tm, 10), _mix_spec(tm, 11), _mix_spec(tm, 12), _mix_spec(tm, 13),
                  _full_spec(w.shape)],
        out_specs=_row_spec(tm, BRANCH_W),
        out_shape=jax.ShapeDtypeStruct((seq, BRANCH_W), BF16),
        scratch_shapes=[pltpu.VMEM((SUBLANES, BRANCH_W), F32)],
        compiler_params=_params(("arbitrary",), VMEM_SMALL),
        name="sconv",
    )(p_mix, p_mix, p_mix, p_mix, w)


def _merge_body(h_ref, y_ref, wg_ref, wb_ref, o_ref, acc_ref):
    g = pl.program_id(2)
    gate = _sigmoid(jnp.dot(h_ref[...], wg_ref[...], preferred_element_type=F32))
    term = gate * jnp.dot(y_ref[0], wb_ref[0], preferred_element_type=F32)

    @pl.when(g == 0)
    def _():
        acc_ref[...] = term

    @pl.when(g > 0)
    def _():
        acc_ref[...] += term

    @pl.when(g == N_BRANCH - 1)
    def _():
        o_ref[...] = acc_ref[...].astype(o_ref.dtype)


def _merge(h, ys, w_g, w_b):
    seq = h.shape[0]
    n_tiles = D_MODEL // TN_MERGE
    return pl.pallas_call(
        _merge_body,
        grid=(seq // TM_MERGE, n_tiles, N_BRANCH),
        in_specs=[pl.BlockSpec((TM_MERGE, D_MODEL), lambda i, j, g: (i, 0)),
                  pl.BlockSpec((1, TM_MERGE, BRANCH_W), lambda i, j, g: (g, i, 0)),
                  pl.BlockSpec((D_MODEL, TN_MERGE), lambda i, j, g: (0, g * n_tiles + j)),
                  pl.BlockSpec((1, BRANCH_W, TN_MERGE), lambda i, j, g: (g, 0, j))],
        out_specs=pl.BlockSpec((TM_MERGE, TN_MERGE), lambda i, j, g: (i, j)),
        out_shape=jax.ShapeDtypeStruct((seq, D_MODEL), BF16),
        scratch_shapes=[pltpu.VMEM((TM_MERGE, TN_MERGE), F32)],
        compiler_params=_params(("parallel", "parallel", "arbitrary"), VMEM_MM),
        name="merge",
    )(h, ys, w_g, w_b)


def _pad_lanes(w):
    return jnp.pad(w, ((0, 0), (0, LANES - w.shape[1])))


def _gate_weights(w_in_l):
    wi = _pad_lanes(w_in_l[:, GATE_COL0:GATE_COL0 + HEADS]).astype(BF16)
    wf = _pad_lanes(w_in_l[:, GATE_COL0 + HEADS:MIX_COL1]).astype(BF16)
    return wi, wf


def kernel(x, pre_w, post_w, w_in, mlstm_gate_bias, mlstm_norm_w, gmlp_ln_w, gmlp_ln_b, gmlp_w_s, gmlp_b_s,
           rglru_conv_w, rglru_conv_b, rglru_w_gate, rglru_b_gate, rglru_a_param, sconv_w, w_branch, w_out):
    batch, seq, _ = x.shape
    depth = w_in.shape[0]
    outs = []
    for bi in range(batch):
        xb = x[bi]
        wi, wf = _gate_weights(w_in[0])
        h, gi, gf = _prenorm(xb, pre_w[0][None], wi, wf)
        for l in range(depth):
            w_l = w_in[l]
            w_mix = jnp.concatenate([w_l[:, :GATE_COL0], w_l[:, MIX_COL1:MERGE_COL0]], axis=1).astype(BF16)
            w_g = w_l[:, MERGE_COL0:].astype(BF16)
            p_mix = _matmul(h, w_mix, F32, "proj")
            bias_i = _pad_lanes(mlstm_gate_bias[l][None, :HEADS])
            bias_f = _pad_lanes(mlstm_gate_bias[l][None, HEADS:])
            y_a = _mlstm(p_mix, gi, gf, bias_i, bias_f, mlstm_norm_w[l][None])
            y_b = _gmlp(p_mix, gmlp_ln_w[l][None], gmlp_ln_b[l][None], gmlp_w_s[l], gmlp_b_s[l].T)
            y_c = _rglru(p_mix, rglru_conv_w[l], rglru_conv_b[l][None], rglru_w_gate[l],
                         rglru_b_gate[l][None], rglru_a_param[l][None])
            y_d = _sconv(p_mix, sconv_w[l])
            ys = jnp.stack([y_a, y_b, y_c, y_d], axis=0)
            merged = _merge(h, ys, w_g, w_branch[l].astype(BF16))
            out = _matmul(merged, w_out[l].astype(BF16), F32, "out_proj")
            if l + 1 < depth:
                wi, wf = _gate_weights(w_in[l + 1])
                xb, h, gi, gf = _post_next(xb, out, post_w[l][None], pre_w[l + 1][None], wi, wf)
            else:
                xb = _post(xb, out, post_w[l][None])
        outs.append(xb)
    return jnp.stack(outs, axis=0)
```

```python
import jax
import jax.numpy as jnp
from jax import lax
from jax.experimental import pallas as pl
from jax.experimental.pallas import tpu as pltpu

F32 = jnp.float32
BF16 = jnp.bfloat16

D_MODEL = 4096
BRANCH_W = D_MODEL // 4
N_BRANCH = 4
HEADS = 8
HEAD_DIM = BRANCH_W // HEADS
CHUNK = 128
GROUPS = 8
GROUP_W = BRANCH_W // GROUPS
LRU_BLOCKS = 8
LRU_BW = BRANCH_W // LRU_BLOCKS
LRU_C = 8.0
EPS = 1e-6

LANES = 128
SUBLANES = 8
N_GATE_COLS = 2 * HEADS
GATE_COL0 = 5 * BRANCH_W
MIX_W = 14 * BRANCH_W
W_ALL = MIX_W + N_BRANCH * D_MODEL

TM_NORM = 256
TM_MM = 1024
TN_MM = 1024
TM_MERGE = 1024
TN_MERGE = 512
TN_OUT = 512
TK_PREP = 1024
TN_PREP = 1024
TM_MLSTM = CHUNK
TM_GMLP = 256
TM_LRU = 256
TM_CONV = 256
VMEM_MM = 52 * 1024 * 1024
VMEM_SMALL = 40 * 1024 * 1024


def _params(semantics, vmem):
    return pltpu.CompilerParams(dimension_semantics=semantics, vmem_limit_bytes=vmem)


def _sigmoid(x):
    return 1.0 / (1.0 + jnp.exp(-x))


def _silu(x):
    return x * _sigmoid(x)


def _softplus(x):
    return jnp.maximum(x, 0.0) + jnp.log1p(jnp.exp(-jnp.abs(x)))


def _expm1(y):
    u = jnp.exp(y)
    um1 = u - 1.0
    lg = jnp.log(u)
    r = um1 * y / jnp.where(lg == 0.0, 1.0, lg)
    return jnp.where(um1 == 0.0, y, jnp.where(um1 == -1.0, -1.0, r))


def _row_iota(shape):
    return lax.broadcasted_iota(jnp.int32, shape, 0)


def _lane_iota(shape):
    return lax.broadcasted_iota(jnp.int32, shape, len(shape) - 1)


def _shift_rows(x, d, fill):
    rows, cols = x.shape
    if d % SUBLANES == 0:
        return jnp.concatenate([jnp.full((d, cols), fill, x.dtype), x[:rows - d]], axis=0)
    return jnp.where(_row_iota(x.shape) >= d, pltpu.roll(x, d, 0), fill)


def _causal_conv(x, prev, w):
    taps = w.shape[0]

    def conv(arr):
        y = pltpu.roll(arr, taps - 1, 0) * w[0:1]
        for k in range(1, taps - 1):
            y = y + pltpu.roll(arr, taps - 1 - k, 0) * w[k:k + 1]
        return y + arr * w[taps - 1:taps]

    body = conv(x)
    head = conv(jnp.concatenate([prev, x[:SUBLANES]], axis=0))[SUBLANES:]
    return jnp.concatenate([head, body[SUBLANES:]], axis=0)


def _rms(x):
    return x * lax.rsqrt(jnp.mean(x * x, axis=-1, keepdims=True) + EPS)


def _row_spec(tm, width):
    return pl.BlockSpec((tm, width), lambda i: (i, 0))


def _full_spec(shape):
    return pl.BlockSpec(shape, lambda *_: (0,) * len(shape))


def _layer_spec(layer, shape):
    return pl.BlockSpec((None,) + shape, lambda *_: (layer,) + (0,) * len(shape))


def _prep_body(a_ref, b_ref, g_ref, w_ref, wg_ref):
    j = pl.program_id(1)
    n_aligned = GATE_COL0 // TN_PREP

    @pl.when(j < n_aligned)
    def _():
        w_ref[...] = a_ref[...].astype(BF16)

    @pl.when(j >= n_aligned)
    def _():
        cat = jnp.concatenate([a_ref[...], b_ref[...]], axis=1)
        w_ref[...] = pltpu.roll(cat, cat.shape[1] - N_GATE_COLS, 1)[:, :TN_PREP].astype(BF16)

    @pl.when(j == 0)
    def _():
        g = g_ref[...]
        wg_ref[...] = jnp.where(_lane_iota(g.shape) < N_GATE_COLS, g, 0.0).astype(BF16)


def _prep(w_in, layer):
    k = w_in.shape[1]
    lanes_per_tile = TN_PREP // LANES
    return pl.pallas_call(
        _prep_body,
        grid=(k // TK_PREP, W_ALL // TN_PREP),
        in_specs=[pl.BlockSpec((None, TK_PREP, TN_PREP), lambda i, j: (layer, i, j)),
                  pl.BlockSpec((None, TK_PREP, LANES), lambda i, j: (layer, i, (j + 1) * lanes_per_tile)),
                  pl.BlockSpec((None, TK_PREP, LANES), lambda i, j: (layer, i, GATE_COL0 // LANES))],
        out_specs=(pl.BlockSpec((TK_PREP, TN_PREP), lambda i, j: (i, j)),
                   pl.BlockSpec((TK_PREP, LANES), lambda i, j: (i, 0))),
        out_shape=(jax.ShapeDtypeStruct((k, W_ALL), BF16), jax.ShapeDtypeStruct((k, LANES), BF16)),
        compiler_params=_params(("parallel", "arbitrary"), VMEM_SMALL),
        name="prep",
    )(w_in, w_in, w_in)


def _norm_proj(x, pw_ref, wg_ref, h_ref, g_ref):
    hb = (_rms(x) * pw_ref[...]).astype(BF16)
    h_ref[...] = hb
    g_ref[...] = jnp.dot(hb, wg_ref[...], preferred_element_type=F32)


def _prenorm_body(x_ref, pw_ref, wg_ref, h_ref, g_ref):
    _norm_proj(x_ref[...], pw_ref, wg_ref, h_ref, g_ref)


def _post_body(x_ref, o_ref, qw_ref, xo_ref):
    xo_ref[...] = x_ref[...] + _rms(o_ref[...]) * qw_ref[...]


def _post_next_body(x_ref, o_ref, qw_ref, pw_ref, wg_ref, xo_ref, h_ref, g_ref):
    xn = x_ref[...] + _rms(o_ref[...]) * qw_ref[...]
    xo_ref[...] = xn
    _norm_proj(xn, pw_ref, wg_ref, h_ref, g_ref)


def _norm_out(seq):
    shapes = (jax.ShapeDtypeStruct((seq, D_MODEL), BF16), jax.ShapeDtypeStruct((seq, LANES), F32))
    specs = (_row_spec(TM_NORM, D_MODEL), _row_spec(TM_NORM, LANES))
    return shapes, specs


def _prenorm(x, pre_w, layer, wg):
    seq = x.shape[0]
    shapes, specs = _norm_out(seq)
    return pl.pallas_call(
        _prenorm_body,
        grid=(seq // TM_NORM,),
        in_specs=[_row_spec(TM_NORM, D_MODEL), _layer_spec(layer, (1, D_MODEL)), _full_spec((D_MODEL, LANES))],
        out_specs=specs, out_shape=shapes,
        compiler_params=_params(("parallel",), VMEM_SMALL),
        name="prenorm",
    )(x, pre_w, wg)


def _post(x, o, post_w, layer):
    seq = x.shape[0]
    return pl.pallas_call(
        _post_body,
        grid=(seq // TM_NORM,),
        in_specs=[_row_spec(TM_NORM, D_MODEL), _row_spec(TM_NORM, D_MODEL), _layer_spec(layer, (1, D_MODEL))],
        out_specs=_row_spec(TM_NORM, D_MODEL),
        out_shape=jax.ShapeDtypeStruct((seq, D_MODEL), F32),
        compiler_params=_params(("parallel",), VMEM_SMALL),
        name="post",
    )(x, o, post_w)


def _post_next(x, o, post_w, pre_w, layer, wg):
    seq = x.shape[0]
    shapes, specs = _norm_out(seq)
    return pl.pallas_call(
        _post_next_body,
        grid=(seq // TM_NORM,),
        in_specs=[_row_spec(TM_NORM, D_MODEL), _row_spec(TM_NORM, D_MODEL), _layer_spec(layer, (1, D_MODEL)),
                  _layer_spec(layer + 1, (1, D_MODEL)), _full_spec((D_MODEL, LANES))],
        out_specs=(_row_spec(TM_NORM, D_MODEL),) + specs,
        out_shape=(jax.ShapeDtypeStruct((seq, D_MODEL), F32),) + shapes,
        compiler_params=_params(("parallel",), VMEM_SMALL),
        name="post_next",
    )(x, o, post_w, pre_w, wg)


def _mm_body(a_ref, b_ref, o_ref):
    o_ref[...] = jnp.dot(a_ref[...], b_ref[...], preferred_element_type=F32).astype(o_ref.dtype)


def _proj(h, w_all):
    m, k = h.shape
    return pl.pallas_call(
        _mm_body,
        grid=(m // TM_MM, MIX_W // TN_MM),
        in_specs=[pl.BlockSpec((TM_MM, k), lambda i, j: (i, 0)),
                  pl.BlockSpec((k, TN_MM), lambda i, j: (0, j))],
        out_specs=pl.BlockSpec((TM_MM, TN_MM), lambda i, j: (i, j)),
        out_shape=jax.ShapeDtypeStruct((m, MIX_W), F32),
        compiler_params=_params(("parallel", "parallel"), VMEM_MM),
        name="proj",
    )(h, w_all)


def _out_body(a_ref, w_ref, o_ref, wb_ref):
    @pl.when(pl.program_id(1) == 0)
    def _():
        wb_ref[...] = w_ref[...].astype(BF16)

    o_ref[...] = jnp.dot(a_ref[...], wb_ref[...], preferred_element_type=F32)


def _out_proj(merged, w_out, layer):
    m, k = merged.shape
    n = w_out.shape[2]
    return pl.pallas_call(
        _out_body,
        grid=(n // TN_OUT, m // TM_MM),
        in_specs=[pl.BlockSpec((TM_MM, k), lambda j, i: (i, 0)),
                  pl.BlockSpec((None, k, TN_OUT), lambda j, i: (layer, 0, j))],
        out_specs=pl.BlockSpec((TM_MM, TN_OUT), lambda j, i: (i, j)),
        out_shape=jax.ShapeDtypeStruct((m, n), F32),
        scratch_shapes=[pltpu.VMEM((k, TN_OUT), BF16)],
        compiler_params=_params(("parallel", "arbitrary"), VMEM_MM),
        name="out_proj",
    )(merged, w_out)


def _mlstm_body(q_ref, k_ref, v_ref, o_ref, z_ref, g_ref, gb_ref, nw_ref, y_ref, state_ref, m_ref):
    @pl.when(pl.program_id(0) == 0)
    def _():
        state_ref[...] = jnp.zeros_like(state_ref)
        m_ref[...] = jnp.zeros_like(m_ref)

    i_pre = g_ref[...] + gb_ref[...]
    f_pre = pltpu.roll(i_pre, LANES - HEADS, 1)
    lf = -_softplus(-f_pre)
    b = lf
    d = 1
    while d < CHUNK:
        b = b + _shift_rows(b, d, 0.0)
        d *= 2
    beta = i_pre - b
    cm = beta
    d = 1
    while d < CHUNK:
        cm = jnp.maximum(cm, _shift_rows(cm, d, -jnp.inf))
        d *= 2
    m_prev = m_ref[...]
    inter = b + m_prev
    m_t = jnp.maximum(inter, b + cm)
    alpha = b - m_t
    w_inter = jnp.exp(inter - m_t)
    e_negm = jnp.exp(-m_t)
    b_last = b[CHUNK - 1:CHUNK]
    m_new = jnp.maximum(b_last + m_prev, b_last + cm[CHUNK - 1:CHUNK])
    decay = jnp.exp(b_last + m_prev - m_new)
    ws = jnp.exp(b_last + beta - m_new)
    m_ref[...] = m_new
    beta_t = beta.T

    shape = (CHUNK, CHUNK)
    tril = _row_iota(shape) >= _lane_iota(shape)
    lane0 = _lane_iota(shape) == 0

    for h in range(HEADS):
        sl = slice(h * HEAD_DIM, (h + 1) * HEAD_DIM)
        q = q_ref[:, sl].astype(BF16)
        k_t = (k_ref[:, sl] * (HEAD_DIM ** -0.5)).T.astype(BF16)
        v = v_ref[:, sl]
        qk = jnp.dot(q, k_t, preferred_element_type=F32)
        dmat = alpha[:, h:h + 1] + beta_t[h:h + 1, :]
        scores = qk * jnp.exp(jnp.where(tril, dmat, -jnp.inf))
        state = state_ref[h]
        q_state = jnp.dot(q, state.astype(BF16), preferred_element_type=F32)
        wi_col = w_inter[:, h:h + 1]
        num = (jnp.dot(scores.astype(BF16), v.astype(BF16), preferred_element_type=F32)
               + wi_col * q_state[:, :HEAD_DIM])
        den = jnp.sum(scores, axis=-1, keepdims=True) + wi_col * q_state[:, HEAD_DIM:HEAD_DIM + 1]
        hh = num / jnp.maximum(jnp.abs(den), e_negm[:, h:h + 1])
        y = _rms(hh) * nw_ref[:, sl]
        y = (y * _sigmoid(o_ref[:, sl])) * _silu(z_ref[:, sl])
        y_ref[:, sl] = y.astype(y_ref.dtype)
        ws_col = ws[:, h:h + 1]
        upd = jnp.concatenate([(ws_col * v).astype(BF16),
                               jnp.where(lane0, ws_col, 0.0).astype(BF16)], axis=1)
        state_ref[h] = decay[:, h:h + 1] * state + jnp.dot(k_t, upd, preferred_element_type=F32)


def _mix_spec(tm, col_block):
    return pl.BlockSpec((tm, BRANCH_W), lambda i: (i, col_block))


def _mlstm(p_mix, gates, gate_bias, norm_w, layer):
    seq = p_mix.shape[0]
    tm = TM_MLSTM
    return pl.pallas_call(
        _mlstm_body,
        grid=(seq // tm,),
        in_specs=[_mix_spec(tm, 0), _mix_spec(tm, 1), _mix_spec(tm, 2), _mix_spec(tm, 3), _mix_spec(tm, 4),
                  _row_spec(tm, LANES), _full_spec((1, LANES)), _layer_spec(layer, (1, BRANCH_W))],
        out_specs=_row_spec(tm, BRANCH_W),
        out_shape=jax.ShapeDtypeStruct((seq, BRANCH_W), BF16),
        scratch_shapes=[pltpu.VMEM((HEADS, HEAD_DIM, 2 * HEAD_DIM), F32), pltpu.VMEM((1, LANES), F32)],
        compiler_params=_params(("arbitrary",), VMEM_SMALL),
        name="mlstm",
    )(p_mix, p_mix, p_mix, p_mix, p_mix, gates, gate_bias, norm_w)


def _gmlp_body(u_ref, v_ref, z_ref, lw_ref, lb_ref, ws_ref, bst_ref, y_ref):
    v = v_ref[...]
    vc = v - jnp.mean(v, axis=-1, keepdims=True)
    var = jnp.mean(vc * vc, axis=-1, keepdims=True)
    vn = (vc * lax.rsqrt(var + EPS) * lw_ref[...] + lb_ref[...]).astype(BF16)
    shape = (CHUNK, CHUNK)
    tril = _row_iota(shape) >= _lane_iota(shape)
    for g in range(GROUPS):
        cols = slice(g * GROUP_W, (g + 1) * GROUP_W)
        w = jnp.where(tril, ws_ref[g], 0.0).astype(BF16)
        bias = bst_ref[:, g:g + 1]
        for c in range(v.shape[0] // CHUNK):
            rows = slice(c * CHUNK, (c + 1) * CHUNK)
            sp = jnp.dot(w, vn[rows, cols], preferred_element_type=F32) + bias
            y = u_ref[rows, cols] * sp * _silu(z_ref[rows, cols])
            y_ref[rows, cols] = y.astype(y_ref.dtype)


def _gmlp(p_mix, ln_w, ln_b, w_s, b_s_t, layer):
    seq = p_mix.shape[0]
    tm = TM_GMLP
    return pl.pallas_call(
        _gmlp_body,
        grid=(seq // tm,),
        in_specs=[_mix_spec(tm, 5), _mix_spec(tm, 6), _mix_spec(tm, 7),
                  _layer_spec(layer, (1, BRANCH_W)), _layer_spec(layer, (1, BRANCH_W)),
                  _layer_spec(layer, (GROUPS, CHUNK, CHUNK)), _full_spec((CHUNK, GROUPS))],
        out_specs=_row_spec(tm, BRANCH_W),
        out_shape=jax.ShapeDtypeStruct((seq, BRANCH_W), BF16),
        compiler_params=_params(("parallel",), VMEM_SMALL),
        name="gmlp",
    )(p_mix, p_mix, p_mix, ln_w, ln_b, w_s, b_s_t)


def _linear_scan(a, b):
    d = 1
    while d < a.shape[0]:
        b = b + a * _shift_rows(b, d, 0.0)
        a = a * _shift_rows(a, d, 1.0)
        d *= 2
    return a, b


def _rglru_body(x_ref, z_ref, cw_ref, cb_ref, wg_ref, bg_ref, ap_ref, y_ref, prev_ref, carry_ref):
    step = pl.program_id(0)

    @pl.when(step == 0)
    def _():
        prev_ref[...] = jnp.zeros_like(prev_ref)
        carry_ref[...] = jnp.zeros_like(carry_ref)

    x = x_ref[...]
    tm = x.shape[0]
    xc = _causal_conv(x, prev_ref[...], cw_ref[...]) + cb_ref[...]
    prev_ref[...] = x[tm - SUBLANES:]
    sp = _softplus(-ap_ref[...])
    seq_start = (_row_iota((tm, LRU_BW)) == 0) & (step == 0)
    for n in range(LRU_BLOCKS):
        cols = slice(n * LRU_BW, (n + 1) * LRU_BW)
        xn = xc[:, cols]
        gt = jnp.dot(xn.astype(BF16), wg_ref[n].astype(BF16), preferred_element_type=F32)
        r = _sigmoid(gt[:, :LRU_BW] + bg_ref[:, cols])
        ig = _sigmoid(gt[:, LRU_BW:] + bg_ref[:, BRANCH_W + n * LRU_BW:BRANCH_W + (n + 1) * LRU_BW])
        log_a = -LRU_C * r * sp[:, cols]
        a = jnp.exp(log_a)
        mult = jnp.sqrt(-_expm1(2.0 * log_a))
        mult = jnp.where(seq_start, 1.0, mult)
        a_cum, hs = _linear_scan(a, mult * ig * xn)
        hs = hs + a_cum * carry_ref[:, cols]
        carry_ref[:, cols] = hs[tm - 1:]
        y_ref[:, cols] = (hs * _silu(z_ref[:, cols])).astype(y_ref.dtype)


def _rglru(p_mix, conv_w, conv_b, w_gate, b_gate, a_param, layer):
    seq = p_mix.shape[0]
    tm = TM_LRU
    return pl.pallas_call(
        _rglru_body,
        grid=(seq // tm,),
        in_specs=[_mix_spec(tm, 8), _mix_spec(tm, 9),
                  _layer_spec(layer, conv_w.shape[1:]), _layer_spec(layer, (1, BRANCH_W)),
                  _layer_spec(layer, w_gate.shape[1:]), _layer_spec(layer, (1, 2 * BRANCH_W)),
                  _layer_spec(layer, (1, BRANCH_W))],
        out_specs=_row_spec(tm, BRANCH_W),
        out_shape=jax.ShapeDtypeStruct((seq, BRANCH_W), BF16),
        scratch_shapes=[pltpu.VMEM((SUBLANES, BRANCH_W), F32), pltpu.VMEM((1, BRANCH_W), F32)],
        compiler_params=_params(("arbitrary",), VMEM_SMALL),
        name="rglru",
    )(p_mix, p_mix, conv_w, conv_b, w_gate, b_gate, a_param)


def _sconv_body(b_ref, c_ref, x_ref, z_ref, w_ref, y_ref, prev_ref):
    @pl.when(pl.program_id(0) == 0)
    def _():
        prev_ref[...] = jnp.zeros_like(prev_ref)

    p = c_ref[...] * x_ref[...]
    conv = _causal_conv(p, prev_ref[...], w_ref[...])
    prev_ref[...] = p[p.shape[0] - SUBLANES:]
    y_ref[...] = (b_ref[...] * conv * _silu(z_ref[...])).astype(y_ref.dtype)


def _sconv(p_mix, w, layer):
    seq = p_mix.shape[0]
    tm = TM_CONV
    return pl.pallas_call(
        _sconv_body,
        grid=(seq // tm,),
        in_specs=[_mix_spec(tm, 10), _mix_spec(tm, 11), _mix_spec(tm, 12), _mix_spec(tm, 13),
                  _layer_spec(layer, w.shape[1:])],
        out_specs=_row_spec(tm, BRANCH_W),
        out_shape=jax.ShapeDtypeStruct((seq, BRANCH_W), BF16),
        scratch_shapes=[pltpu.VMEM((SUBLANES, BRANCH_W), F32)],
        compiler_params=_params(("arbitrary",), VMEM_SMALL),
        name="sconv",
    )(p_mix, p_mix, p_mix, p_mix, w)


def _merge_body(h_ref, y_ref, wg_ref, wb_ref, o_ref, acc_ref):
    g = pl.program_id(2)
    gate = _sigmoid(jnp.dot(h_ref[...], wg_ref[...], preferred_element_type=F32))
    term = gate * jnp.dot(y_ref[...], wb_ref[...].astype(BF16), preferred_element_type=F32)

    @pl.when(g == 0)
    def _():
        acc_ref[...] = term

    @pl.when(g > 0)
    def _():
        acc_ref[...] += term

    @pl.when(g == N_BRANCH - 1)
    def _():
        o_ref[...] = acc_ref[...].astype(o_ref.dtype)


def _merge(h, ys, w_all, w_branch, layer):
    seq = h.shape[0]
    n_tiles = D_MODEL // TN_MERGE
    col0 = MIX_W // TN_MERGE
    return pl.pallas_call(
        _merge_body,
        grid=(seq // TM_MERGE, n_tiles, N_BRANCH),
        in_specs=[pl.BlockSpec((TM_MERGE, D_MODEL), lambda i, j, g: (i, 0)),
                  pl.BlockSpec((None, TM_MERGE, BRANCH_W), lambda i, j, g: (g, i, 0)),
                  pl.BlockSpec((D_MODEL, TN_MERGE), lambda i, j, g: (0, col0 + g * n_tiles + j)),
                  pl.BlockSpec((None, None, BRANCH_W, TN_MERGE), lambda i, j, g: (layer, g, 0, j))],
        out_specs=pl.BlockSpec((TM_MERGE, TN_MERGE), lambda i, j, g: (i, j)),
        out_shape=jax.ShapeDtypeStruct((seq, D_MODEL), BF16),
        scratch_shapes=[pltpu.VMEM((TM_MERGE, TN_MERGE), F32)],
        compiler_params=_params(("parallel", "parallel", "arbitrary"), VMEM_MM),
        name="merge",
    )(h, ys, w_all, w_branch)


def kernel(x, pre_w, post_w, w_in, mlstm_gate_bias, mlstm_norm_w, gmlp_ln_w, gmlp_ln_b, gmlp_w_s, gmlp_b_s,
           rglru_conv_w, rglru_conv_b, rglru_w_gate, rglru_b_gate, rglru_a_param, sconv_w, w_branch, w_out):
    batch = x.shape[0]
    depth = w_in.shape[0]
    row = lambda p: p[:, None, :]
    pre_w, post_w = row(pre_w), row(post_w)
    norm_w, ln_w, ln_b = row(mlstm_norm_w), row(gmlp_ln_w), row(gmlp_ln_b)
    conv_b, b_gate, a_param = row(rglru_conv_b), row(rglru_b_gate), row(rglru_a_param)
    gate_bias = jnp.pad(mlstm_gate_bias, ((0, 0), (0, LANES - N_GATE_COLS)))
    b_s_t = jnp.swapaxes(gmlp_b_s, 1, 2)
    weights = [_prep(w_in, l) for l in range(depth)]
    outs = []
    for bi in range(batch):
        xb = x[bi]
        h, gates = _prenorm(xb, pre_w, 0, weights[0][1])
        for l in range(depth):
            w_all = weights[l][0]
            p_mix = _proj(h, w_all)
            y_a = _mlstm(p_mix, gates, gate_bias[l][None], norm_w, l)
            y_b = _gmlp(p_mix, ln_w, ln_b, gmlp_w_s, b_s_t[l], l)
            y_c = _rglru(p_mix, rglru_conv_w, conv_b, rglru_w_gate, b_gate, a_param, l)
            y_d = _sconv(p_mix, sconv_w, l)
            ys = jnp.stack([y_a, y_b, y_c, y_d], axis=0)
            merged = _merge(h, ys, w_all, w_branch, l)
            out = _out_proj(merged, w_out, l)
            if l + 1 < depth:
                xb, h, gates = _post_next(xb, out, post_w, pre_w, l, weights[l + 1][1])
            else:
                xb = _post(xb, out, post_w, l)
        outs.append(xb)
    return jnp.stack(outs, axis=0)
```

```python
import jax
import jax.numpy as jnp
from jax import lax
from jax.experimental import pallas as pl
from jax.experimental.pallas import tpu as pltpu

F32 = jnp.float32
BF16 = jnp.bfloat16

D_MODEL = 4096
BRANCH_W = D_MODEL // 4
N_BRANCH = 4
HEADS = 8
HEAD_DIM = BRANCH_W // HEADS
CHUNK = 128
GROUPS = 8
GROUP_W = BRANCH_W // GROUPS
LRU_BLOCKS = 8
LRU_BW = BRANCH_W // LRU_BLOCKS
LRU_C = 8.0
EPS = 1e-6

LANES = 128
SUBLANES = 8
N_GATE_COLS = 2 * HEADS
GATE_COL0 = 5 * BRANCH_W
MIX_W = 14 * BRANCH_W
W_ALL = MIX_W + N_BRANCH * D_MODEL

TM_NORM = 256
TM_MM = 1024
TN_MM = 1024
TM_MERGE = 1024
TN_MERGE = 512
TN_OUT = 512
TK_PREP = 1024
TN_PREP = 1024
TM_MLSTM = CHUNK
TM_GMLP = 256
TM_LRU = 256
TM_CONV = 256
VMEM_MM = 52 * 1024 * 1024
VMEM_SMALL = 40 * 1024 * 1024


def _params(semantics, vmem):
    return pltpu.CompilerParams(dimension_semantics=semantics, vmem_limit_bytes=vmem)


def _sigmoid(x):
    return 1.0 / (1.0 + jnp.exp(-x))


def _silu(x):
    return x * _sigmoid(x)


def _softplus(x):
    return jnp.maximum(x, 0.0) + jnp.log1p(jnp.exp(-jnp.abs(x)))


def _expm1(y):
    u = jnp.exp(y)
    um1 = u - 1.0
    lg = jnp.log(u)
    r = um1 * y / jnp.where(lg == 0.0, 1.0, lg)
    return jnp.where(um1 == 0.0, y, jnp.where(um1 == -1.0, -1.0, r))


def _row_iota(shape):
    return lax.broadcasted_iota(jnp.int32, shape, 0)


def _lane_iota(shape):
    return lax.broadcasted_iota(jnp.int32, shape, len(shape) - 1)


def _shift_rows(x, d, fill):
    rows, cols = x.shape
    if d % SUBLANES == 0:
        return jnp.concatenate([jnp.full((d, cols), fill, x.dtype), x[:rows - d]], axis=0)
    return jnp.where(_row_iota(x.shape) >= d, pltpu.roll(x, d, 0), fill)


def _causal_conv(x, prev, w):
    taps = w.shape[0]

    def conv(arr):
        y = pltpu.roll(arr, taps - 1, 0) * w[0:1]
        for k in range(1, taps - 1):
            y = y + pltpu.roll(arr, taps - 1 - k, 0) * w[k:k + 1]
        return y + arr * w[taps - 1:taps]

    body = conv(x)
    head = conv(jnp.concatenate([prev, x[:SUBLANES]], axis=0))[SUBLANES:]
    return jnp.concatenate([head, body[SUBLANES:]], axis=0)


def _rms(x):
    return x * lax.rsqrt(jnp.mean(x * x, axis=-1, keepdims=True) + EPS)


def _row_spec(tm, width):
    return pl.BlockSpec((tm, width), lambda i: (i, 0))


def _full_spec(shape):
    return pl.BlockSpec(shape, lambda *_: (0,) * len(shape))


def _layer_spec(layer, shape):
    return pl.BlockSpec((None,) + shape, lambda *_: (layer,) + (0,) * len(shape))


def _prep_body(a_ref, g_ref, w_ref, wg_ref):
    w_ref[...] = a_ref[...].T.astype(BF16)

    @pl.when(pl.program_id(1) == 0)
    def _():
        g = g_ref[...]
        wg_ref[...] = jnp.where(_row_iota(g.shape) < N_GATE_COLS, g, 0.0).T.astype(BF16)


def _prep(w_in_t, layer):
    k = w_in_t.shape[2]
    n_aligned = GATE_COL0 // TN_PREP

    def src_rows(j):
        octets = j * (TN_PREP // SUBLANES) + jnp.where(j >= n_aligned, N_GATE_COLS // SUBLANES, 0)
        return octets * SUBLANES

    return pl.pallas_call(
        _prep_body,
        grid=(k // TK_PREP, W_ALL // TN_PREP),
        in_specs=[pl.BlockSpec((None, pl.Element(TN_PREP), pl.Element(TK_PREP)),
                               lambda i, j: (layer, src_rows(j), i * TK_PREP)),
                  pl.BlockSpec((None, pl.Element(LANES), pl.Element(TK_PREP)),
                               lambda i, j: (layer, GATE_COL0, i * TK_PREP))],
        out_specs=(pl.BlockSpec((TK_PREP, TN_PREP), lambda i, j: (i, j)),
                   pl.BlockSpec((TK_PREP, LANES), lambda i, j: (i, 0))),
        out_shape=(jax.ShapeDtypeStruct((k, W_ALL), BF16), jax.ShapeDtypeStruct((k, LANES), BF16)),
        compiler_params=_params(("parallel", "arbitrary"), VMEM_SMALL),
        name="prep",
    )(w_in_t, w_in_t)


def _norm_proj(x, pw_ref, wg_ref, h_ref, g_ref):
    hb = (_rms(x) * pw_ref[...]).astype(BF16)
    h_ref[...] = hb
    g_ref[...] = jnp.dot(hb, wg_ref[...], preferred_element_type=F32)


def _prenorm_body(x_ref, pw_ref, wg_ref, h_ref, g_ref):
    _norm_proj(x_ref[...], pw_ref, wg_ref, h_ref, g_ref)


def _post_body(x_ref, o_ref, qw_ref, xo_ref):
    xo_ref[...] = x_ref[...] + _rms(o_ref[...]) * qw_ref[...]


def _post_next_body(x_ref, o_ref, qw_ref, pw_ref, wg_ref, xo_ref, h_ref, g_ref):
    xn = x_ref[...] + _rms(o_ref[...]) * qw_ref[...]
    xo_ref[...] = xn
    _norm_proj(xn, pw_ref, wg_ref, h_ref, g_ref)


def _norm_out(seq):
    shapes = (jax.ShapeDtypeStruct((seq, D_MODEL), BF16), jax.ShapeDtypeStruct((seq, LANES), F32))
    specs = (_row_spec(TM_NORM, D_MODEL), _row_spec(TM_NORM, LANES))
    return shapes, specs


def _prenorm(x, pre_w, layer, wg):
    seq = x.shape[0]
    shapes, specs = _norm_out(seq)
    return pl.pallas_call(
        _prenorm_body,
        grid=(seq // TM_NORM,),
        in_specs=[_row_spec(TM_NORM, D_MODEL), _layer_spec(layer, (1, D_MODEL)), _full_spec((D_MODEL, LANES))],
        out_specs=specs, out_shape=shapes,
        compiler_params=_params(("parallel",), VMEM_SMALL),
        name="prenorm",
    )(x, pre_w, wg)


def _post(x, o, post_w, layer):
    seq = x.shape[0]
    return pl.pallas_call(
        _post_body,
        grid=(seq // TM_NORM,),
        in_specs=[_row_spec(TM_NORM, D_MODEL), _row_spec(TM_NORM, D_MODEL), _layer_spec(layer, (1, D_MODEL))],
        out_specs=_row_spec(TM_NORM, D_MODEL),
        out_shape=jax.ShapeDtypeStruct((seq, D_MODEL), F32),
        compiler_params=_params(("parallel",), VMEM_SMALL),
        name="post",
    )(x, o, post_w)


def _post_next(x, o, post_w, pre_w, layer, wg):
    seq = x.shape[0]
    shapes, specs = _norm_out(seq)
    return pl.pallas_call(
        _post_next_body,
        grid=(seq // TM_NORM,),
        in_specs=[_row_spec(TM_NORM, D_MODEL), _row_spec(TM_NORM, D_MODEL), _layer_spec(layer, (1, D_MODEL)),
                  _layer_spec(layer + 1, (1, D_MODEL)), _full_spec((D_MODEL, LANES))],
        out_specs=(_row_spec(TM_NORM, D_MODEL),) + specs,
        out_shape=(jax.ShapeDtypeStruct((seq, D_MODEL), F32),) + shapes,
        compiler_params=_params(("parallel",), VMEM_SMALL),
        name="post_next",
    )(x, o, post_w, pre_w, wg)


def _mm_body(a_ref, b_ref, o_ref):
    o_ref[...] = jnp.dot(a_ref[...], b_ref[...], preferred_element_type=F32).astype(o_ref.dtype)


def _proj(h, w_all):
    m, k = h.shape
    return pl.pallas_call(
        _mm_body,
        grid=(m // TM_MM, MIX_W // TN_MM),
        in_specs=[pl.BlockSpec((TM_MM, k), lambda i, j: (i, 0)),
                  pl.BlockSpec((k, TN_MM), lambda i, j: (0, j))],
        out_specs=pl.BlockSpec((TM_MM, TN_MM), lambda i, j: (i, j)),
        out_shape=jax.ShapeDtypeStruct((m, MIX_W), F32),
        compiler_params=_params(("parallel", "parallel"), VMEM_MM),
        name="proj",
    )(h, w_all)


def _out_body(a_ref, w_ref, o_ref, wb_ref):
    @pl.when(pl.program_id(1) == 0)
    def _():
        wb_ref[...] = w_ref[...].astype(BF16)

    o_ref[...] = jnp.dot(a_ref[...], wb_ref[...], preferred_element_type=F32)


def _out_proj(merged, w_out, layer):
    m, k = merged.shape
    n = w_out.shape[2]
    return pl.pallas_call(
        _out_body,
        grid=(n // TN_OUT, m // TM_MM),
        in_specs=[pl.BlockSpec((TM_MM, k), lambda j, i: (i, 0)),
                  pl.BlockSpec((None, k, TN_OUT), lambda j, i: (layer, 0, j))],
        out_specs=pl.BlockSpec((TM_MM, TN_OUT), lambda j, i: (i, j)),
        out_shape=jax.ShapeDtypeStruct((m, n), F32),
        scratch_shapes=[pltpu.VMEM((k, TN_OUT), BF16)],
        compiler_params=_params(("parallel", "arbitrary"), VMEM_MM),
        name="out_proj",
    )(merged, w_out)


def _mlstm_body(q_ref, k_ref, v_ref, o_ref, z_ref, g_ref, gb_ref, nw_ref, y_ref, state_ref, m_ref):
    @pl.when(pl.program_id(0) == 0)
    def _():
        state_ref[...] = jnp.zeros_like(state_ref)
        m_ref[...] = jnp.zeros_like(m_ref)

    i_pre = g_ref[...] + gb_ref[...]
    f_pre = pltpu.roll(i_pre, LANES - HEADS, 1)
    lf = -_softplus(-f_pre)
    b = lf
    d = 1
    while d < CHUNK:
        b = b + _shift_rows(b, d, 0.0)
        d *= 2
    beta = i_pre - b
    cm = beta
    d = 1
    while d < CHUNK:
        cm = jnp.maximum(cm, _shift_rows(cm, d, -jnp.inf))
        d *= 2
    m_prev = m_ref[...]
    inter = b + m_prev
    m_t = jnp.maximum(inter, b + cm)
    alpha = b - m_t
    w_inter = jnp.exp(inter - m_t)
    e_negm = jnp.exp(-m_t)
    b_last = b[CHUNK - 1:CHUNK]
    m_new = jnp.maximum(b_last + m_prev, b_last + cm[CHUNK - 1:CHUNK])
    decay = jnp.exp(b_last + m_prev - m_new)
    ws = jnp.exp(b_last + beta - m_new)
    m_ref[...] = m_new
    beta_t = beta.T

    shape = (CHUNK, CHUNK)
    tril = _row_iota(shape) >= _lane_iota(shape)
    lane0 = _lane_iota(shape) == 0

    for h in range(HEADS):
        sl = slice(h * HEAD_DIM, (h + 1) * HEAD_DIM)
        q = q_ref[:, sl].astype(BF16)
        k_t = (k_ref[:, sl] * (HEAD_DIM ** -0.5)).T.astype(BF16)
        v = v_ref[:, sl]
        qk = jnp.dot(q, k_t, preferred_element_type=F32)
        dmat = alpha[:, h:h + 1] + beta_t[h:h + 1, :]
        scores = qk * jnp.exp(jnp.where(tril, dmat, -jnp.inf))
        state = state_ref[h]
        q_state = jnp.dot(q, state.astype(BF16), preferred_element_type=F32)
        wi_col = w_inter[:, h:h + 1]
        num = (jnp.dot(scores.astype(BF16), v.astype(BF16), preferred_element_type=F32)
               + wi_col * q_state[:, :HEAD_DIM])
        den = jnp.sum(scores, axis=-1, keepdims=True) + wi_col * q_state[:, HEAD_DIM:HEAD_DIM + 1]
        hh = num / jnp.maximum(jnp.abs(den), e_negm[:, h:h + 1])
        y = _rms(hh) * nw_ref[:, sl]
        y = (y * _sigmoid(o_ref[:, sl])) * _silu(z_ref[:, sl])
        y_ref[:, sl] = y.astype(y_ref.dtype)
        ws_col = ws[:, h:h + 1]
        upd = jnp.concatenate([(ws_col * v).astype(BF16),
                               jnp.where(lane0, ws_col, 0.0).astype(BF16)], axis=1)
        state_ref[h] = decay[:, h:h + 1] * state + jnp.dot(k_t, upd, preferred_element_type=F32)


def _mix_spec(tm, col_block):
    return pl.BlockSpec((tm, BRANCH_W), lambda i: (i, col_block))


def _mlstm(p_mix, gates, gate_bias, norm_w, layer):
    seq = p_mix.shape[0]
    tm = TM_MLSTM
    return pl.pallas_call(
        _mlstm_body,
        grid=(seq // tm,),
        in_specs=[_mix_spec(tm, 0), _mix_spec(tm, 1), _mix_spec(tm, 2), _mix_spec(tm, 3), _mix_spec(tm, 4),
                  _row_spec(tm, LANES), _full_spec((1, LANES)), _layer_spec(layer, (1, BRANCH_W))],
        out_specs=_row_spec(tm, BRANCH_W),
        out_shape=jax.ShapeDtypeStruct((seq, BRANCH_W), BF16),
        scratch_shapes=[pltpu.VMEM((HEADS, HEAD_DIM, 2 * HEAD_DIM), F32), pltpu.VMEM((1, LANES), F32)],
        compiler_params=_params(("arbitrary",), VMEM_SMALL),
        name="mlstm",
    )(p_mix, p_mix, p_mix, p_mix, p_mix, gates, gate_bias, norm_w)


def _gmlp_body(u_ref, v_ref, z_ref, lw_ref, lb_ref, ws_ref, bst_ref, y_ref):
    v = v_ref[...]
    vc = v - jnp.mean(v, axis=-1, keepdims=True)
    var = jnp.mean(vc * vc, axis=-1, keepdims=True)
    vn = (vc * lax.rsqrt(var + EPS) * lw_ref[...] + lb_ref[...]).astype(BF16)
    shape = (CHUNK, CHUNK)
    tril = _row_iota(shape) >= _lane_iota(shape)
    for g in range(GROUPS):
        cols = slice(g * GROUP_W, (g + 1) * GROUP_W)
        w = jnp.where(tril, ws_ref[g], 0.0).astype(BF16)
        bias = bst_ref[:, g:g + 1]
        for c in range(v.shape[0] // CHUNK):
            rows = slice(c * CHUNK, (c + 1) * CHUNK)
            sp = jnp.dot(w, vn[rows, cols], preferred_element_type=F32) + bias
            y = u_ref[rows, cols] * sp * _silu(z_ref[rows, cols])
            y_ref[rows, cols] = y.astype(y_ref.dtype)


def _gmlp(p_mix, ln_w, ln_b, w_s, b_s_t, layer):
    seq = p_mix.shape[0]
    tm = TM_GMLP
    return pl.pallas_call(
        _gmlp_body,
        grid=(seq // tm,),
        in_specs=[_mix_spec(tm, 5), _mix_spec(tm, 6), _mix_spec(tm, 7),
                  _layer_spec(layer, (1, BRANCH_W)), _layer_spec(layer, (1, BRANCH_W)),
                  _layer_spec(layer, (GROUPS, CHUNK, CHUNK)), _full_spec((CHUNK, GROUPS))],
        out_specs=_row_spec(tm, BRANCH_W),
        out_shape=jax.ShapeDtypeStruct((seq, BRANCH_W), BF16),
        compiler_params=_params(("parallel",), VMEM_SMALL),
        name="gmlp",
    )(p_mix, p_mix, p_mix, ln_w, ln_b, w_s, b_s_t)


def _linear_scan(a, b):
    d = 1
    while d < a.shape[0]:
        b = b + a * _shift_rows(b, d, 0.0)
        a = a * _shift_rows(a, d, 1.0)
        d *= 2
    return a, b


def _rglru_body(x_ref, z_ref, cw_ref, cb_ref, wg_ref, bg_ref, ap_ref, y_ref, prev_ref, carry_ref):
    step = pl.program_id(0)

    @pl.when(step == 0)
    def _():
        prev_ref[...] = jnp.zeros_like(prev_ref)
        carry_ref[...] = jnp.zeros_like(carry_ref)

    x = x_ref[...]
    tm = x.shape[0]
    xc = _causal_conv(x, prev_ref[...], cw_ref[...]) + cb_ref[...]
    prev_ref[...] = x[tm - SUBLANES:]
    sp = _softplus(-ap_ref[...])
    seq_start = (_row_iota((tm, LRU_BW)) == 0) & (step == 0)
    for n in range(LRU_BLOCKS):
        cols = slice(n * LRU_BW, (n + 1) * LRU_BW)
        xn = xc[:, cols]
        gt = jnp.dot(xn.astype(BF16), wg_ref[n].astype(BF16), preferred_element_type=F32)
        r = _sigmoid(gt[:, :LRU_BW] + bg_ref[:, cols])
        ig = _sigmoid(gt[:, LRU_BW:] + bg_ref[:, BRANCH_W + n * LRU_BW:BRANCH_W + (n + 1) * LRU_BW])
        log_a = -LRU_C * r * sp[:, cols]
        a = jnp.exp(log_a)
        mult = jnp.sqrt(-_expm1(2.0 * log_a))
        mult = jnp.where(seq_start, 1.0, mult)
        a_cum, hs = _linear_scan(a, mult * ig * xn)
        hs = hs + a_cum * carry_ref[:, cols]
        carry_ref[:, cols] = hs[tm - 1:]
        y_ref[:, cols] = (hs * _silu(z_ref[:, cols])).astype(y_ref.dtype)


def _rglru(p_mix, conv_w, conv_b, w_gate, b_gate, a_param, layer):
    seq = p_mix.shape[0]
    tm = TM_LRU
    return pl.pallas_call(
        _rglru_body,
        grid=(seq // tm,),
        in_specs=[_mix_spec(tm, 8), _mix_spec(tm, 9),
                  _layer_spec(layer, conv_w.shape[1:]), _layer_spec(layer, (1, BRANCH_W)),
                  _layer_spec(layer, w_gate.shape[1:]), _layer_spec(layer, (1, 2 * BRANCH_W)),
                  _layer_spec(layer, (1, BRANCH_W))],
        out_specs=_row_spec(tm, BRANCH_W),
        out_shape=jax.ShapeDtypeStruct((seq, BRANCH_W), BF16),
        scratch_shapes=[pltpu.VMEM((SUBLANES, BRANCH_W), F32), pltpu.VMEM((1, BRANCH_W), F32)],
        compiler_params=_params(("arbitrary",), VMEM_SMALL),
        name="rglru",
    )(p_mix, p_mix, conv_w, conv_b, w_gate, b_gate, a_param)


def _sconv_body(b_ref, c_ref, x_ref, z_ref, w_ref, y_ref, prev_ref):
    @pl.when(pl.program_id(0) == 0)
    def _():
        prev_ref[...] = jnp.zeros_like(prev_ref)

    p = c_ref[...] * x_ref[...]
    conv = _causal_conv(p, prev_ref[...], w_ref[...])
    prev_ref[...] = p[p.shape[0] - SUBLANES:]
    y_ref[...] = (b_ref[...] * conv * _silu(z_ref[...])).astype(y_ref.dtype)


def _sconv(p_mix, w, layer):
    seq = p_mix.shape[0]
    tm = TM_CONV
    return pl.pallas_call(
        _sconv_body,
        grid=(seq // tm,),
        in_specs=[_mix_spec(tm, 10), _mix_spec(tm, 11), _mix_spec(tm, 12), _mix_spec(tm, 13),
                  _layer_spec(layer, w.shape[1:])],
        out_specs=_row_spec(tm, BRANCH_W),
        out_shape=jax.ShapeDtypeStruct((seq, BRANCH_W), BF16),
        scratch_shapes=[pltpu.VMEM((SUBLANES, BRANCH_W), F32)],
        compiler_params=_params(("arbitrary",), VMEM_SMALL),
        name="sconv",
    )(p_mix, p_mix, p_mix, p_mix, w)


def _merge_body(h_ref, y_ref, wg_ref, wb_ref, o_ref, acc_ref):
    g = pl.program_id(2)
    gate = _sigmoid(jnp.dot(h_ref[...], wg_ref[...], preferred_element_type=F32))
    term = gate * jnp.dot(y_ref[...], wb_ref[...].astype(BF16), preferred_element_type=F32)

    @pl.when(g == 0)
    def _():
        acc_ref[...] = term

    @pl.when(g > 0)
    def _():
        acc_ref[...] += term

    @pl.when(g == N_BRANCH - 1)
    def _():
        o_ref[...] = acc_ref[...].astype(o_ref.dtype)


def _merge(h, ys, w_all, w_branch, layer):
    seq = h.shape[0]
    n_tiles = D_MODEL // TN_MERGE
    col0 = MIX_W // TN_MERGE
    return pl.pallas_call(
        _merge_body,
        grid=(seq // TM_MERGE, n_tiles, N_BRANCH),
        in_specs=[pl.BlockSpec((TM_MERGE, D_MODEL), lambda i, j, g: (i, 0)),
                  pl.BlockSpec((None, TM_MERGE, BRANCH_W), lambda i, j, g: (g, i, 0)),
                  pl.BlockSpec((D_MODEL, TN_MERGE), lambda i, j, g: (0, col0 + g * n_tiles + j)),
                  pl.BlockSpec((None, None, BRANCH_W, TN_MERGE), lambda i, j, g: (layer, g, 0, j))],
        out_specs=pl.BlockSpec((TM_MERGE, TN_MERGE), lambda i, j, g: (i, j)),
        out_shape=jax.ShapeDtypeStruct((seq, D_MODEL), BF16),
        scratch_shapes=[pltpu.VMEM((TM_MERGE, TN_MERGE), F32)],
        compiler_params=_params(("parallel", "parallel", "arbitrary"), VMEM_MM),
        name="merge",
    )(h, ys, w_all, w_branch)


def kernel(x, pre_w, post_w, w_in, mlstm_gate_bias, mlstm_norm_w, gmlp_ln_w, gmlp_ln_b, gmlp_w_s, gmlp_b_s,
           rglru_conv_w, rglru_conv_b, rglru_w_gate, rglru_b_gate, rglru_a_param, sconv_w, w_branch, w_out):
    batch = x.shape[0]
    depth = w_in.shape[0]
    row = lambda p: p[:, None, :]
    pre_w, post_w = row(pre_w), row(post_w)
    norm_w, ln_w, ln_b = row(mlstm_norm_w), row(gmlp_ln_w), row(gmlp_ln_b)
    conv_b, b_gate, a_param = row(rglru_conv_b), row(rglru_b_gate), row(rglru_a_param)
    gate_bias = jnp.pad(mlstm_gate_bias, ((0, 0), (0, LANES - N_GATE_COLS)))
    b_s_t = jnp.swapaxes(gmlp_b_s, 1, 2)
    w_in_t = jnp.swapaxes(w_in, 1, 2)
    weights = [_prep(w_in_t, l) for l in range(depth)]
    outs = []
    for bi in range(batch):
        xb = x[bi]
        h, gates = _prenorm(xb, pre_w, 0, weights[0][1])
        for l in range(depth):
            w_all = weights[l][0]
            p_mix = _proj(h, w_all)
            y_a = _mlstm(p_mix, gates, gate_bias[l][None], norm_w, l)
            y_b = _gmlp(p_mix, ln_w, ln_b, gmlp_w_s, b_s_t[l], l)
            y_c = _rglru(p_mix, rglru_conv_w, conv_b, rglru_w_gate, b_gate, a_param, l)
            y_d = _sconv(p_mix, sconv_w, l)
            ys = jnp.stack([y_a, y_b, y_c, y_d], axis=0)
            merged = _merge(h, ys, w_all, w_branch, l)
            out = _out_proj(merged, w_out, l)
            if l + 1 < depth:
                xb, h, gates = _post_next(xb, out, post_w, pre_w, l, weights[l + 1][1])
            else:
                xb = _post(xb, out, post_w, l)
        outs.append(xb)
    return jnp.stack(outs, axis=0)
```

```python
import jax
import jax.numpy as jnp
from jax import lax
from jax.experimental import pallas as pl
from jax.experimental.pallas import tpu as pltpu

F32 = jnp.float32
BF16 = jnp.bfloat16

D_MODEL = 4096
BRANCH_W = D_MODEL // 4
N_BRANCH = 4
HEADS = 8
HEAD_DIM = BRANCH_W // HEADS
CHUNK = 128
GROUPS = 8
GROUP_W = BRANCH_W // GROUPS
LRU_BLOCKS = 8
LRU_BW = BRANCH_W // LRU_BLOCKS
LRU_C = 8.0
EPS = 1e-6

LANES = 128
SUBLANES = 8
N_GATE_COLS = 2 * HEADS
GATE_COL0 = 5 * BRANCH_W
MIX_W = 14 * BRANCH_W

TM_NORM = 256
TM_PROJ = 2048
TN_PROJ = 512
TM_MERGE = 1024
TN_MERGE = 512
TM_OUT = 1024
TN_OUT = 512
TM_MLSTM = CHUNK
TM_GMLP = 256
TM_LRU = 256
TM_CONV = 256
VMEM_MM = 52 * 1024 * 1024
VMEM_SMALL = 40 * 1024 * 1024


def _params(semantics, vmem):
    return pltpu.CompilerParams(dimension_semantics=semantics, vmem_limit_bytes=vmem)


def _sigmoid(x):
    return 1.0 / (1.0 + jnp.exp(-x))


def _silu(x):
    return x * _sigmoid(x)


def _softplus(x):
    return jnp.maximum(x, 0.0) + jnp.log1p(jnp.exp(-jnp.abs(x)))


def _expm1(y):
    u = jnp.exp(y)
    um1 = u - 1.0
    lg = jnp.log(u)
    r = um1 * y / jnp.where(lg == 0.0, 1.0, lg)
    return jnp.where(um1 == 0.0, y, jnp.where(um1 == -1.0, -1.0, r))


def _row_iota(shape):
    return lax.broadcasted_iota(jnp.int32, shape, 0)


def _lane_iota(shape):
    return lax.broadcasted_iota(jnp.int32, shape, len(shape) - 1)


def _shift_rows(x, d, fill):
    rows, cols = x.shape
    if d % SUBLANES == 0:
        return jnp.concatenate([jnp.full((d, cols), fill, x.dtype), x[:rows - d]], axis=0)
    return jnp.where(_row_iota(x.shape) >= d, pltpu.roll(x, d, 0), fill)


def _causal_conv(x, prev, w):
    taps = w.shape[0]

    def conv(arr):
        y = pltpu.roll(arr, taps - 1, 0) * w[0:1]
        for k in range(1, taps - 1):
            y = y + pltpu.roll(arr, taps - 1 - k, 0) * w[k:k + 1]
        return y + arr * w[taps - 1:taps]

    body = conv(x)
    head = conv(jnp.concatenate([prev, x[:SUBLANES]], axis=0))[SUBLANES:]
    return jnp.concatenate([head, body[SUBLANES:]], axis=0)


def _rms(x):
    return x * lax.rsqrt(jnp.mean(x * x, axis=-1, keepdims=True) + EPS)


def _row_spec(tm, width):
    return pl.BlockSpec((tm, width), lambda i: (i, 0))


def _full_spec(shape):
    return pl.BlockSpec(shape, lambda *_: (0,) * len(shape))


def _layer_spec(layer, shape):
    return pl.BlockSpec((None,) + shape, lambda *_: (layer,) + (0,) * len(shape))


def _dot_nt(a, b_t):
    return lax.dot_general(a, b_t, (((1,), (1,)), ((), ())), preferred_element_type=F32)


def _aligned_rows(octets):
    return octets * SUBLANES


def _wt_spec(layer, rows, row_start):
    return pl.BlockSpec((None, pl.Element(rows), pl.Element(D_MODEL)),
                        lambda *idx: (layer, row_start(*idx), 0))


def _gate_rows(w):
    return jnp.where(_row_iota(w.shape) < N_GATE_COLS, w, 0.0).astype(BF16)


def _norm_proj(x, pw_ref, wg_ref, h_ref, g_ref):
    hb = (_rms(x) * pw_ref[...]).astype(BF16)
    h_ref[...] = hb
    g_ref[...] = _dot_nt(hb, _gate_rows(wg_ref[...]))


def _prenorm_body(x_ref, pw_ref, wg_ref, h_ref, g_ref):
    _norm_proj(x_ref[...], pw_ref, wg_ref, h_ref, g_ref)


def _post_body(x_ref, o_ref, qw_ref, xo_ref):
    xo_ref[...] = x_ref[...] + _rms(o_ref[...]) * qw_ref[...]


def _post_next_body(x_ref, o_ref, qw_ref, pw_ref, wg_ref, xo_ref, h_ref, g_ref):
    xn = x_ref[...] + _rms(o_ref[...]) * qw_ref[...]
    xo_ref[...] = xn
    _norm_proj(xn, pw_ref, wg_ref, h_ref, g_ref)


def _norm_out(seq):
    shapes = (jax.ShapeDtypeStruct((seq, D_MODEL), BF16), jax.ShapeDtypeStruct((seq, LANES), F32))
    specs = (_row_spec(TM_NORM, D_MODEL), _row_spec(TM_NORM, LANES))
    return shapes, specs


def _gate_spec(layer):
    return _wt_spec(layer, LANES, lambda *_: GATE_COL0)


def _prenorm(x, pre_w, layer, w_in_t):
    seq = x.shape[0]
    shapes, specs = _norm_out(seq)
    return pl.pallas_call(
        _prenorm_body,
        grid=(seq // TM_NORM,),
        in_specs=[_row_spec(TM_NORM, D_MODEL), _layer_spec(layer, (1, D_MODEL)), _gate_spec(layer)],
        out_specs=specs, out_shape=shapes,
        compiler_params=_params(("parallel",), VMEM_SMALL),
        name="prenorm",
    )(x, pre_w, w_in_t)


def _post(x, o, post_w, layer):
    seq = x.shape[0]
    return pl.pallas_call(
        _post_body,
        grid=(seq // TM_NORM,),
        in_specs=[_row_spec(TM_NORM, D_MODEL), _row_spec(TM_NORM, D_MODEL), _layer_spec(layer, (1, D_MODEL))],
        out_specs=_row_spec(TM_NORM, D_MODEL),
        out_shape=jax.ShapeDtypeStruct((seq, D_MODEL), F32),
        compiler_params=_params(("parallel",), VMEM_SMALL),
        name="post",
    )(x, o, post_w)


def _post_next(x, o, post_w, pre_w, layer, w_in_t):
    seq = x.shape[0]
    shapes, specs = _norm_out(seq)
    return pl.pallas_call(
        _post_next_body,
        grid=(seq // TM_NORM,),
        in_specs=[_row_spec(TM_NORM, D_MODEL), _row_spec(TM_NORM, D_MODEL), _layer_spec(layer, (1, D_MODEL)),
                  _layer_spec(layer + 1, (1, D_MODEL)), _gate_spec(layer + 1)],
        out_specs=(_row_spec(TM_NORM, D_MODEL),) + specs,
        out_shape=(jax.ShapeDtypeStruct((seq, D_MODEL), F32),) + shapes,
        compiler_params=_params(("parallel",), VMEM_SMALL),
        name="post_next",
    )(x, o, post_w, pre_w, w_in_t)


def _proj_rows(i, j):
    return _aligned_rows(j * (TN_PROJ // SUBLANES)
                         + jnp.where(j >= GATE_COL0 // TN_PROJ, N_GATE_COLS // SUBLANES, 0))


def _proj_body(h_ref, w_ref, o_ref):
    o_ref[...] = _dot_nt(h_ref[...], w_ref[...].astype(BF16))


def _proj(h, w_in_t, layer):
    m, k = h.shape
    return pl.pallas_call(
        _proj_body,
        grid=(m // TM_PROJ, MIX_W // TN_PROJ),
        in_specs=[pl.BlockSpec((TM_PROJ, k), lambda i, j: (i, 0), pipeline_mode=pl.Buffered(1)),
                  _wt_spec(layer, TN_PROJ, _proj_rows)],
        out_specs=pl.BlockSpec((TM_PROJ, TN_PROJ), lambda i, j: (i, j)),
        out_shape=jax.ShapeDtypeStruct((m, MIX_W), F32),
        compiler_params=_params(("parallel", "parallel"), VMEM_MM),
        name="proj",
    )(h, w_in_t)


def _out_body(a_ref, w_ref, o_ref, wb_ref):
    @pl.when(pl.program_id(1) == 0)
    def _():
        wb_ref[...] = w_ref[...].astype(BF16)

    o_ref[...] = jnp.dot(a_ref[...], wb_ref[...], preferred_element_type=F32)


def _out_proj(merged, w_out, layer):
    m, k = merged.shape
    n = w_out.shape[2]
    return pl.pallas_call(
        _out_body,
        grid=(n // TN_OUT, m // TM_OUT),
        in_specs=[pl.BlockSpec((TM_OUT, k), lambda j, i: (i, 0)),
                  pl.BlockSpec((None, k, TN_OUT), lambda j, i: (layer, 0, j))],
        out_specs=pl.BlockSpec((TM_OUT, TN_OUT), lambda j, i: (i, j)),
        out_shape=jax.ShapeDtypeStruct((m, n), F32),
        scratch_shapes=[pltpu.VMEM((k, TN_OUT), BF16)],
        compiler_params=_params(("parallel", "arbitrary"), VMEM_MM),
        name="out_proj",
    )(merged, w_out)


def _mlstm_body(q_ref, k_ref, v_ref, o_ref, z_ref, g_ref, gb_ref, nw_ref, y_ref, state_ref, m_ref):
    @pl.when(pl.program_id(0) == 0)
    def _():
        state_ref[...] = jnp.zeros_like(state_ref)
        m_ref[...] = jnp.zeros_like(m_ref)

    i_pre = g_ref[...] + gb_ref[...]
    f_pre = pltpu.roll(i_pre, LANES - HEADS, 1)
    lf = -_softplus(-f_pre)
    b = lf
    d = 1
    while d < CHUNK:
        b = b + _shift_rows(b, d, 0.0)
        d *= 2
    beta = i_pre - b
    cm = beta
    d = 1
    while d < CHUNK:
        cm = jnp.maximum(cm, _shift_rows(cm, d, -jnp.inf))
        d *= 2
    m_prev = m_ref[...]
    inter = b + m_prev
    m_t = jnp.maximum(inter, b + cm)
    alpha = b - m_t
    w_inter = jnp.exp(inter - m_t)
    e_negm = jnp.exp(-m_t)
    b_last = b[CHUNK - 1:CHUNK]
    m_new = jnp.maximum(b_last + m_prev, b_last + cm[CHUNK - 1:CHUNK])
    decay = jnp.exp(b_last + m_prev - m_new)
    ws = jnp.exp(b_last + beta - m_new)
    m_ref[...] = m_new
    beta_t = beta.T

    shape = (CHUNK, CHUNK)
    tril = _row_iota(shape) >= _lane_iota(shape)
    lane0 = _lane_iota(shape) == 0

    for h in range(HEADS):
        sl = slice(h * HEAD_DIM, (h + 1) * HEAD_DIM)
        q = q_ref[:, sl].astype(BF16)
        k_t = (k_ref[:, sl] * (HEAD_DIM ** -0.5)).T.astype(BF16)
        v = v_ref[:, sl]
        qk = jnp.dot(q, k_t, preferred_element_type=F32)
        dmat = alpha[:, h:h + 1] + beta_t[h:h + 1, :]
        scores = qk * jnp.exp(jnp.where(tril, dmat, -jnp.inf))
        state = state_ref[h]
        q_state = jnp.dot(q, state.astype(BF16), preferred_element_type=F32)
        wi_col = w_inter[:, h:h + 1]
        num = (jnp.dot(scores.astype(BF16), v.astype(BF16), preferred_element_type=F32)
               + wi_col * q_state[:, :HEAD_DIM])
        den = jnp.sum(scores, axis=-1, keepdims=True) + wi_col * q_state[:, HEAD_DIM:HEAD_DIM + 1]
        hh = num / jnp.maximum(jnp.abs(den), e_negm[:, h:h + 1])
        y = _rms(hh) * nw_ref[:, sl]
        y = (y * _sigmoid(o_ref[:, sl])) * _silu(z_ref[:, sl])
        y_ref[:, sl] = y.astype(y_ref.dtype)
        ws_col = ws[:, h:h + 1]
        upd = jnp.concatenate([(ws_col * v).astype(BF16),
                               jnp.where(lane0, ws_col, 0.0).astype(BF16)], axis=1)
        state_ref[h] = decay[:, h:h + 1] * state + jnp.dot(k_t, upd, preferred_element_type=F32)


def _mix_spec(tm, col_block):
    return pl.BlockSpec((tm, BRANCH_W), lambda i: (i, col_block))


def _ys_shape(seq):
    return jax.ShapeDtypeStruct((N_BRANCH, seq, BRANCH_W), BF16)


def _ys_spec(tm, branch):
    return pl.BlockSpec((None, tm, BRANCH_W), lambda i: (branch, i, 0))


_YS_ALIAS_SPEC = pl.BlockSpec(memory_space=pl.ANY)


def _mlstm(p_mix, gates, gate_bias, norm_w, layer):
    seq = p_mix.shape[0]
    tm = TM_MLSTM
    return pl.pallas_call(
        _mlstm_body,
        grid=(seq // tm,),
        in_specs=[_mix_spec(tm, 0), _mix_spec(tm, 1), _mix_spec(tm, 2), _mix_spec(tm, 3), _mix_spec(tm, 4),
                  _row_spec(tm, LANES), _full_spec((1, LANES)), _layer_spec(layer, (1, BRANCH_W))],
        out_specs=_ys_spec(tm, 0),
        out_shape=_ys_shape(seq),
        scratch_shapes=[pltpu.VMEM((HEADS, HEAD_DIM, 2 * HEAD_DIM), F32), pltpu.VMEM((1, LANES), F32)],
        compiler_params=_params(("arbitrary",), VMEM_SMALL),
        name="mlstm",
    )(p_mix, p_mix, p_mix, p_mix, p_mix, gates, gate_bias, norm_w)


def _gmlp_body(u_ref, v_ref, z_ref, lw_ref, lb_ref, ws_ref, bst_ref, _ys_ref, y_ref):
    v = v_ref[...]
    vc = v - jnp.mean(v, axis=-1, keepdims=True)
    var = jnp.mean(vc * vc, axis=-1, keepdims=True)
    vn = (vc * lax.rsqrt(var + EPS) * lw_ref[...] + lb_ref[...]).astype(BF16)
    shape = (CHUNK, CHUNK)
    tril = _row_iota(shape) >= _lane_iota(shape)
    for g in range(GROUPS):
        cols = slice(g * GROUP_W, (g + 1) * GROUP_W)
        w = jnp.where(tril, ws_ref[g], 0.0).astype(BF16)
        bias = bst_ref[:, g:g + 1]
        for c in range(v.shape[0] // CHUNK):
            rows = slice(c * CHUNK, (c + 1) * CHUNK)
            sp = jnp.dot(w, vn[rows, cols], preferred_element_type=F32) + bias
            y = u_ref[rows, cols] * sp * _silu(z_ref[rows, cols])
            y_ref[rows, cols] = y.astype(y_ref.dtype)


def _gmlp(p_mix, ln_w, ln_b, w_s, b_s_t, layer, ys):
    seq = p_mix.shape[0]
    tm = TM_GMLP
    return pl.pallas_call(
        _gmlp_body,
        grid=(seq // tm,),
        in_specs=[_mix_spec(tm, 5), _mix_spec(tm, 6), _mix_spec(tm, 7),
                  _layer_spec(layer, (1, BRANCH_W)), _layer_spec(layer, (1, BRANCH_W)),
                  _layer_spec(layer, (GROUPS, CHUNK, CHUNK)), _full_spec((CHUNK, GROUPS)), _YS_ALIAS_SPEC],
        out_specs=_ys_spec(tm, 1),
        out_shape=_ys_shape(seq),
        input_output_aliases={7: 0},
        compiler_params=_params(("parallel",), VMEM_SMALL),
        name="gmlp",
    )(p_mix, p_mix, p_mix, ln_w, ln_b, w_s, b_s_t, ys)


def _linear_scan(a, b):
    d = 1
    while d < a.shape[0]:
        b = b + a * _shift_rows(b, d, 0.0)
        a = a * _shift_rows(a, d, 1.0)
        d *= 2
    return a, b


def _rglru_body(x_ref, z_ref, cw_ref, cb_ref, wg_ref, bg_ref, ap_ref, _ys_ref, y_ref, prev_ref, carry_ref):
    step = pl.program_id(0)

    @pl.when(step == 0)
    def _():
        prev_ref[...] = jnp.zeros_like(prev_ref)
        carry_ref[...] = jnp.zeros_like(carry_ref)

    x = x_ref[...]
    tm = x.shape[0]
    xc = _causal_conv(x, prev_ref[...], cw_ref[...]) + cb_ref[...]
    prev_ref[...] = x[tm - SUBLANES:]
    sp = _softplus(-ap_ref[...])
    seq_start = (_row_iota((tm, LRU_BW)) == 0) & (step == 0)
    for n in range(LRU_BLOCKS):
        cols = slice(n * LRU_BW, (n + 1) * LRU_BW)
        xn = xc[:, cols]
        gt = jnp.dot(xn.astype(BF16), wg_ref[n].astype(BF16), preferred_element_type=F32)
        r = _sigmoid(gt[:, :LRU_BW] + bg_ref[:, cols])
        ig = _sigmoid(gt[:, LRU_BW:] + bg_ref[:, BRANCH_W + n * LRU_BW:BRANCH_W + (n + 1) * LRU_BW])
        log_a = -LRU_C * r * sp[:, cols]
        a = jnp.exp(log_a)
        mult = jnp.sqrt(-_expm1(2.0 * log_a))
        mult = jnp.where(seq_start, 1.0, mult)
        a_cum, hs = _linear_scan(a, mult * ig * xn)
        hs = hs + a_cum * carry_ref[:, cols]
        carry_ref[:, cols] = hs[tm - 1:]
        y_ref[:, cols] = (hs * _silu(z_ref[:, cols])).astype(y_ref.dtype)


def _rglru(p_mix, conv_w, conv_b, w_gate, b_gate, a_param, layer, ys):
    seq = p_mix.shape[0]
    tm = TM_LRU
    return pl.pallas_call(
        _rglru_body,
        grid=(seq // tm,),
        in_specs=[_mix_spec(tm, 8), _mix_spec(tm, 9),
                  _layer_spec(layer, conv_w.shape[1:]), _layer_spec(layer, (1, BRANCH_W)),
                  _layer_spec(layer, w_gate.shape[1:]), _layer_spec(layer, (1, 2 * BRANCH_W)),
                  _layer_spec(layer, (1, BRANCH_W)), _YS_ALIAS_SPEC],
        out_specs=_ys_spec(tm, 2),
        out_shape=_ys_shape(seq),
        input_output_aliases={7: 0},
        scratch_shapes=[pltpu.VMEM((SUBLANES, BRANCH_W), F32), pltpu.VMEM((1, BRANCH_W), F32)],
        compiler_params=_params(("arbitrary",), VMEM_SMALL),
        name="rglru",
    )(p_mix, p_mix, conv_w, conv_b, w_gate, b_gate, a_param, ys)


def _sconv_body(b_ref, c_ref, x_ref, z_ref, w_ref, _ys_ref, y_ref, prev_ref):
    @pl.when(pl.program_id(0) == 0)
    def _():
        prev_ref[...] = jnp.zeros_like(prev_ref)

    p = c_ref[...] * x_ref[...]
    conv = _causal_conv(p, prev_ref[...], w_ref[...])
    prev_ref[...] = p[p.shape[0] - SUBLANES:]
    y_ref[...] = (b_ref[...] * conv * _silu(z_ref[...])).astype(y_ref.dtype)


def _sconv(p_mix, w, layer, ys):
    seq = p_mix.shape[0]
    tm = TM_CONV
    return pl.pallas_call(
        _sconv_body,
        grid=(seq // tm,),
        in_specs=[_mix_spec(tm, 10), _mix_spec(tm, 11), _mix_spec(tm, 12), _mix_spec(tm, 13),
                  _layer_spec(layer, w.shape[1:]), _YS_ALIAS_SPEC],
        out_specs=_ys_spec(tm, 3),
        out_shape=_ys_shape(seq),
        input_output_aliases={5: 0},
        scratch_shapes=[pltpu.VMEM((SUBLANES, BRANCH_W), F32)],
        compiler_params=_params(("arbitrary",), VMEM_SMALL),
        name="sconv",
    )(p_mix, p_mix, p_mix, p_mix, w, ys)


def _merge_body(h_ref, y_ref, wg_ref, wb_ref, o_ref, acc_ref):
    g = pl.program_id(2)
    gate = _sigmoid(_dot_nt(h_ref[...], wg_ref[...].astype(BF16)))
    term = gate * jnp.dot(y_ref[...], wb_ref[...].astype(BF16), preferred_element_type=F32)

    @pl.when(g == 0)
    def _():
        acc_ref[...] = term

    @pl.when(g > 0)
    def _():
        acc_ref[...] += term

    @pl.when(g == N_BRANCH - 1)
    def _():
        o_ref[...] = acc_ref[...].astype(o_ref.dtype)


def _merge(h, ys, w_in_t, w_branch, layer):
    seq = h.shape[0]
    n_tiles = D_MODEL // TN_MERGE

    def gate_rows(i, j, g):
        return _aligned_rows((MIX_W + N_GATE_COLS) // SUBLANES + (g * n_tiles + j) * (TN_MERGE // SUBLANES))

    return pl.pallas_call(
        _merge_body,
        grid=(seq // TM_MERGE, n_tiles, N_BRANCH),
        in_specs=[pl.BlockSpec((TM_MERGE, D_MODEL), lambda i, j, g: (i, 0), pipeline_mode=pl.Buffered(1)),
                  pl.BlockSpec((None, TM_MERGE, BRANCH_W), lambda i, j, g: (g, i, 0)),
                  _wt_spec(layer, TN_MERGE, gate_rows),
                  pl.BlockSpec((None, None, BRANCH_W, TN_MERGE), lambda i, j, g: (layer, g, 0, j))],
        out_specs=pl.BlockSpec((TM_MERGE, TN_MERGE), lambda i, j, g: (i, j)),
        out_shape=jax.ShapeDtypeStruct((seq, D_MODEL), BF16),
        scratch_shapes=[pltpu.VMEM((TM_MERGE, TN_MERGE), F32)],
        compiler_params=_params(("parallel", "parallel", "arbitrary"), VMEM_MM),
        name="merge",
    )(h, ys, w_in_t, w_branch)


def kernel(x, pre_w, post_w, w_in, mlstm_gate_bias, mlstm_norm_w, gmlp_ln_w, gmlp_ln_b, gmlp_w_s, gmlp_b_s,
           rglru_conv_w, rglru_conv_b, rglru_w_gate, rglru_b_gate, rglru_a_param, sconv_w, w_branch, w_out):
    batch = x.shape[0]
    depth = w_in.shape[0]
    row = lambda p: p[:, None, :]
    pre_w, post_w = row(pre_w), row(post_w)
    norm_w, ln_w, ln_b = row(mlstm_norm_w), row(gmlp_ln_w), row(gmlp_ln_b)
    conv_b, b_gate, a_param = row(rglru_conv_b), row(rglru_b_gate), row(rglru_a_param)
    gate_bias = jnp.pad(mlstm_gate_bias, ((0, 0), (0, LANES - N_GATE_COLS)))
    b_s_t = jnp.swapaxes(gmlp_b_s, 1, 2)
    w_in_t = jnp.swapaxes(w_in, 1, 2)
    outs = []
    for bi in range(batch):
        xb = x[bi]
        h, gates = _prenorm(xb, pre_w, 0, w_in_t)
        for l in range(depth):
            p_mix = _proj(h, w_in_t, l)
            ys = _mlstm(p_mix, gates, gate_bias[l][None], norm_w, l)
            ys = _gmlp(p_mix, ln_w, ln_b, gmlp_w_s, b_s_t[l], l, ys)
            ys = _rglru(p_mix, rglru_conv_w, conv_b, rglru_w_gate, b_gate, a_param, l, ys)
            ys = _sconv(p_mix, sconv_w, l, ys)
            merged = _merge(h, ys, w_in_t, w_branch, l)
            out = _out_proj(merged, w_out, l)
            if l + 1 < depth:
                xb, h, gates = _post_next(xb, out, post_w, pre_w, l, w_in_t)
            else:
                xb = _post(xb, out, post_w, l)
        outs.append(xb)
    return jnp.stack(outs, axis=0)
```

```python
import jax
import jax.numpy as jnp
from jax import lax
from jax.experimental import pallas as pl
from jax.experimental.pallas import tpu as pltpu

F32 = jnp.float32
BF16 = jnp.bfloat16

D_MODEL = 4096
BRANCH_W = D_MODEL // 4
N_BRANCH = 4
HEADS = 8
HEAD_DIM = BRANCH_W // HEADS
CHUNK = 128
GROUPS = 8
GROUP_W = BRANCH_W // GROUPS
LRU_BLOCKS = 8
LRU_BW = BRANCH_W // LRU_BLOCKS
LRU_C = 8.0
EPS = 1e-6

LANES = 128
SUBLANES = 8
N_GATE_COLS = 2 * HEADS
GATE_COL0 = 5 * BRANCH_W
MIX_W = 14 * BRANCH_W

TM_NORM = 256
TM_PROJ = 2048
TN_PROJ = 512
TM_MERGE = 1024
TN_MERGE = 512
TM_OUT = 2048
TN_OUT = 512
TM_MLSTM = 2 * CHUNK
TM_GMLP = 256
TM_LRU = 256
TM_CONV = 256
VMEM_MM = 52 * 1024 * 1024
VMEM_SMALL = 40 * 1024 * 1024


def _params(semantics, vmem):
    return pltpu.CompilerParams(dimension_semantics=semantics, vmem_limit_bytes=vmem)


def _sigmoid(x):
    return 1.0 / (1.0 + jnp.exp(-x))


def _silu(x):
    return x * _sigmoid(x)


def _softplus(x):
    return jnp.maximum(x, 0.0) + jnp.log1p(jnp.exp(-jnp.abs(x)))


def _row_iota(shape):
    return lax.broadcasted_iota(jnp.int32, shape, 0)


def _lane_iota(shape):
    return lax.broadcasted_iota(jnp.int32, shape, len(shape) - 1)


def _shift_rows(x, d, fill):
    rows, cols = x.shape
    if d % SUBLANES == 0:
        return jnp.concatenate([jnp.full((d, cols), fill, x.dtype), x[:rows - d]], axis=0)
    return jnp.where(_row_iota(x.shape) >= d, pltpu.roll(x, d, 0), fill)


def _causal_conv(x, prev, w):
    taps = w.shape[0]

    def conv(arr):
        y = pltpu.roll(arr, taps - 1, 0) * w[0:1]
        for k in range(1, taps - 1):
            y = y + pltpu.roll(arr, taps - 1 - k, 0) * w[k:k + 1]
        return y + arr * w[taps - 1:taps]

    body = conv(x)
    head = conv(jnp.concatenate([prev, x[:SUBLANES]], axis=0))[SUBLANES:]
    return jnp.concatenate([head, body[SUBLANES:]], axis=0)


def _rms(x):
    return x * lax.rsqrt(jnp.mean(x * x, axis=-1, keepdims=True) + EPS)


def _row_spec(tm, width):
    return pl.BlockSpec((tm, width), lambda i: (i, 0))


def _full_spec(shape):
    return pl.BlockSpec(shape, lambda *_: (0,) * len(shape))


def _layer_spec(layer, shape):
    return pl.BlockSpec((None,) + shape, lambda *_: (layer,) + (0,) * len(shape))


def _dot_nt(a, b_t):
    return lax.dot_general(a, b_t, (((1,), (1,)), ((), ())), preferred_element_type=F32)


def _aligned_rows(octets):
    return octets * SUBLANES


def _wt_spec(layer, rows, row_start):
    return pl.BlockSpec((None, pl.Element(rows), pl.Element(D_MODEL)),
                        lambda *idx: (layer, row_start(*idx), 0))


def _gate_rows(w):
    return jnp.where(_row_iota(w.shape) < N_GATE_COLS, w, 0.0).astype(BF16)


def _norm_proj(x, pw_ref, wg_ref, h_ref, g_ref):
    hb = (_rms(x) * pw_ref[...]).astype(BF16)
    h_ref[...] = hb
    g_ref[...] = _dot_nt(hb, _gate_rows(wg_ref[...]))


def _prenorm_body(x_ref, pw_ref, wg_ref, h_ref, g_ref):
    _norm_proj(x_ref[...], pw_ref, wg_ref, h_ref, g_ref)


def _post_body(x_ref, o_ref, qw_ref, xo_ref):
    xo_ref[...] = x_ref[...] + _rms(o_ref[...]) * qw_ref[...]


def _post_next_body(x_ref, o_ref, qw_ref, pw_ref, wg_ref, xo_ref, h_ref, g_ref):
    xn = x_ref[...] + _rms(o_ref[...]) * qw_ref[...]
    xo_ref[...] = xn
    _norm_proj(xn, pw_ref, wg_ref, h_ref, g_ref)


def _norm_out(seq):
    shapes = (jax.ShapeDtypeStruct((seq, D_MODEL), BF16), jax.ShapeDtypeStruct((seq, LANES), F32))
    specs = (_row_spec(TM_NORM, D_MODEL), _row_spec(TM_NORM, LANES))
    return shapes, specs


def _gate_spec(layer):
    return _wt_spec(layer, LANES, lambda *_: GATE_COL0)


def _prenorm(x, pre_w, layer, w_in_t):
    seq = x.shape[0]
    shapes, specs = _norm_out(seq)
    return pl.pallas_call(
        _prenorm_body,
        grid=(seq // TM_NORM,),
        in_specs=[_row_spec(TM_NORM, D_MODEL), _layer_spec(layer, (1, D_MODEL)), _gate_spec(layer)],
        out_specs=specs, out_shape=shapes,
        compiler_params=_params(("parallel",), VMEM_SMALL),
        name="prenorm",
    )(x, pre_w, w_in_t)


def _post(x, o, post_w, layer):
    seq = x.shape[0]
    return pl.pallas_call(
        _post_body,
        grid=(seq // TM_NORM,),
        in_specs=[_row_spec(TM_NORM, D_MODEL), _row_spec(TM_NORM, D_MODEL), _layer_spec(layer, (1, D_MODEL))],
        out_specs=_row_spec(TM_NORM, D_MODEL),
        out_shape=jax.ShapeDtypeStruct((seq, D_MODEL), F32),
        compiler_params=_params(("parallel",), VMEM_SMALL),
        name="post",
    )(x, o, post_w)


def _post_next(x, o, post_w, pre_w, layer, w_in_t):
    seq = x.shape[0]
    shapes, specs = _norm_out(seq)
    return pl.pallas_call(
        _post_next_body,
        grid=(seq // TM_NORM,),
        in_specs=[_row_spec(TM_NORM, D_MODEL), _row_spec(TM_NORM, D_MODEL), _layer_spec(layer, (1, D_MODEL)),
                  _layer_spec(layer + 1, (1, D_MODEL)), _gate_spec(layer + 1)],
        out_specs=(_row_spec(TM_NORM, D_MODEL),) + specs,
        out_shape=(jax.ShapeDtypeStruct((seq, D_MODEL), F32),) + shapes,
        compiler_params=_params(("parallel",), VMEM_SMALL),
        name="post_next",
    )(x, o, post_w, pre_w, w_in_t)


def _proj_rows(i, j):
    return _aligned_rows(j * (TN_PROJ // SUBLANES)
                         + jnp.where(j >= GATE_COL0 // TN_PROJ, N_GATE_COLS // SUBLANES, 0))


def _proj_body(h_ref, w_ref, o_ref):
    o_ref[...] = _dot_nt(h_ref[...], w_ref[...].astype(BF16))


def _proj(h, w_in_t, layer):
    m, k = h.shape
    return pl.pallas_call(
        _proj_body,
        grid=(m // TM_PROJ, MIX_W // TN_PROJ),
        in_specs=[pl.BlockSpec((TM_PROJ, k), lambda i, j: (i, 0), pipeline_mode=pl.Buffered(1)),
                  _wt_spec(layer, TN_PROJ, _proj_rows)],
        out_specs=pl.BlockSpec((TM_PROJ, TN_PROJ), lambda i, j: (i, j)),
        out_shape=jax.ShapeDtypeStruct((m, MIX_W), F32),
        compiler_params=_params(("parallel", "parallel"), VMEM_MM),
        name="proj",
    )(h, w_in_t)


def _out_body(a_ref, w_ref, o_ref):
    o_ref[...] = jnp.dot(a_ref[...], w_ref[...].astype(BF16), preferred_element_type=F32)


def _out_proj(merged, w_out, layer):
    m, k = merged.shape
    n = w_out.shape[2]
    return pl.pallas_call(
        _out_body,
        grid=(m // TM_OUT, n // TN_OUT),
        in_specs=[pl.BlockSpec((TM_OUT, k), lambda i, j: (i, 0), pipeline_mode=pl.Buffered(1)),
                  pl.BlockSpec((None, k, TN_OUT), lambda i, j: (layer, 0, j))],
        out_specs=pl.BlockSpec((TM_OUT, TN_OUT), lambda i, j: (i, j)),
        out_shape=jax.ShapeDtypeStruct((m, n), F32),
        compiler_params=_params(("parallel", "parallel"), VMEM_MM),
        name="out_proj",
    )(merged, w_out)


def _mlstm_chunk(rows, m_prev, q_ref, k_ref, v_ref, o_ref, z_ref, g_ref, gb_ref, nw_ref, y_ref, state_ref):
    i_pre = g_ref[rows, :] + gb_ref[...]
    f_pre = pltpu.roll(i_pre, LANES - HEADS, 1)
    lf = -_softplus(-f_pre)
    b = lf
    d = 1
    while d < CHUNK:
        b = b + _shift_rows(b, d, 0.0)
        d *= 2
    beta = i_pre - b
    cm = beta
    d = 1
    while d < CHUNK:
        cm = jnp.maximum(cm, _shift_rows(cm, d, -jnp.inf))
        d *= 2
    inter = b + m_prev
    m_t = jnp.maximum(inter, b + cm)
    alpha = b - m_t
    w_inter = jnp.exp(inter - m_t)
    e_negm = jnp.exp(-m_t)
    b_last = b[CHUNK - 1:CHUNK]
    m_new = jnp.maximum(b_last + m_prev, b_last + cm[CHUNK - 1:CHUNK])
    decay = jnp.exp(b_last + m_prev - m_new)
    ws = jnp.exp(b_last + beta - m_new)
    beta_t = beta.T

    shape = (CHUNK, CHUNK)
    tril = _row_iota(shape) >= _lane_iota(shape)
    lane0 = _lane_iota(shape) == 0

    heads = range(HEADS)
    sls = [slice(h * HEAD_DIM, (h + 1) * HEAD_DIM) for h in heads]
    qs = [q_ref[rows, sl].astype(BF16) for sl in sls]
    k_ts = [(k_ref[rows, sl] * (HEAD_DIM ** -0.5)).T.astype(BF16) for sl in sls]
    qks = [jnp.dot(qs[h], k_ts[h], preferred_element_type=F32) for h in heads]
    q_states = [jnp.dot(qs[h], state_ref[h].astype(BF16), preferred_element_type=F32) for h in heads]
    scores = [qks[h] * jnp.exp(jnp.where(tril, alpha[:, h:h + 1] + beta_t[h:h + 1, :], -jnp.inf))
              for h in heads]
    pvs = [jnp.dot(scores[h].astype(BF16), v_ref[rows, sls[h]].astype(BF16), preferred_element_type=F32)
           for h in heads]
    wi_cols = [w_inter[:, h:h + 1] for h in heads]
    dens = [jnp.sum(scores[h], axis=-1, keepdims=True) + wi_cols[h] * q_states[h][:, HEAD_DIM:HEAD_DIM + 1]
            for h in heads]
    r_dens = [1.0 / jnp.maximum(jnp.abs(dens[h]), e_negm[:, h:h + 1]) for h in heads]
    hhs = [(pvs[h] + wi_cols[h] * q_states[h][:, :HEAD_DIM]) * r_dens[h] for h in heads]
    r_norms = [lax.rsqrt(jnp.mean(hh * hh, axis=-1, keepdims=True) + EPS) for hh in hhs]
    for h in heads:
        sl = sls[h]
        y = hhs[h] * r_norms[h] * nw_ref[:, sl]
        y = (y * _sigmoid(o_ref[rows, sl])) * _silu(z_ref[rows, sl])
        y_ref[rows, sl] = y.astype(y_ref.dtype)
    for h in heads:
        ws_col = ws[:, h:h + 1]
        upd = jnp.concatenate([(ws_col * v_ref[rows, sls[h]]).astype(BF16),
                               jnp.where(lane0, ws_col, 0.0).astype(BF16)], axis=1)
        state_ref[h] = decay[:, h:h + 1] * state_ref[h] + jnp.dot(k_ts[h], upd, preferred_element_type=F32)
    return m_new


def _mlstm_body(q_ref, k_ref, v_ref, o_ref, z_ref, g_ref, gb_ref, nw_ref, y_ref, state_ref, m_ref):
    @pl.when(pl.program_id(0) == 0)
    def _():
        state_ref[...] = jnp.zeros_like(state_ref)
        m_ref[...] = jnp.zeros_like(m_ref)

    m = m_ref[...]
    for c in range(q_ref.shape[0] // CHUNK):
        rows = slice(c * CHUNK, (c + 1) * CHUNK)
        m = _mlstm_chunk(rows, m, q_ref, k_ref, v_ref, o_ref, z_ref, g_ref, gb_ref, nw_ref, y_ref, state_ref)
    m_ref[...] = m


def _mix_spec(tm, col_block):
    return pl.BlockSpec((tm, BRANCH_W), lambda i: (i, col_block))


def _ys_shape(seq):
    return jax.ShapeDtypeStruct((N_BRANCH, seq, BRANCH_W), BF16)


def _ys_spec(tm, branch):
    return pl.BlockSpec((None, tm, BRANCH_W), lambda i: (branch, i, 0))


_YS_ALIAS_SPEC = pl.BlockSpec(memory_space=pl.ANY)


def _mlstm(p_mix, gates, gate_bias, norm_w, layer):
    seq = p_mix.shape[0]
    tm = TM_MLSTM
    return pl.pallas_call(
        _mlstm_body,
        grid=(seq // tm,),
        in_specs=[_mix_spec(tm, 0), _mix_spec(tm, 1), _mix_spec(tm, 2), _mix_spec(tm, 3), _mix_spec(tm, 4),
                  _row_spec(tm, LANES), _full_spec((1, LANES)), _layer_spec(layer, (1, BRANCH_W))],
        out_specs=_ys_spec(tm, 0),
        out_shape=_ys_shape(seq),
        scratch_shapes=[pltpu.VMEM((HEADS, HEAD_DIM, 2 * HEAD_DIM), F32), pltpu.VMEM((1, LANES), F32)],
        compiler_params=_params(("arbitrary",), VMEM_SMALL),
        name="mlstm",
    )(p_mix, p_mix, p_mix, p_mix, p_mix, gates, gate_bias, norm_w)


def _gmlp_body(u_ref, v_ref, z_ref, lw_ref, lb_ref, ws_ref, bst_ref, _ys_ref, y_ref):
    v = v_ref[...]
    vc = v - jnp.mean(v, axis=-1, keepdims=True)
    var = jnp.mean(vc * vc, axis=-1, keepdims=True)
    vn = (vc * lax.rsqrt(var + EPS) * lw_ref[...] + lb_ref[...]).astype(BF16)
    shape = (CHUNK, CHUNK)
    tril = _row_iota(shape) >= _lane_iota(shape)
    for g in range(GROUPS):
        cols = slice(g * GROUP_W, (g + 1) * GROUP_W)
        w = jnp.where(tril, ws_ref[g], 0.0).astype(BF16)
        bias = bst_ref[:, g:g + 1]
        for c in range(v.shape[0] // CHUNK):
            rows = slice(c * CHUNK, (c + 1) * CHUNK)
            sp = jnp.dot(w, vn[rows, cols], preferred_element_type=F32) + bias
            y = u_ref[rows, cols] * sp * _silu(z_ref[rows, cols])
            y_ref[rows, cols] = y.astype(y_ref.dtype)


def _gmlp(p_mix, ln_w, ln_b, w_s, b_s_t, layer, ys):
    seq = p_mix.shape[0]
    tm = TM_GMLP
    return pl.pallas_call(
        _gmlp_body,
        grid=(seq // tm,),
        in_specs=[_mix_spec(tm, 5), _mix_spec(tm, 6), _mix_spec(tm, 7),
                  _layer_spec(layer, (1, BRANCH_W)), _layer_spec(layer, (1, BRANCH_W)),
                  _layer_spec(layer, (GROUPS, CHUNK, CHUNK)), _full_spec((CHUNK, GROUPS)), _YS_ALIAS_SPEC],
        out_specs=_ys_spec(tm, 1),
        out_shape=_ys_shape(seq),
        input_output_aliases={7: 0},
        compiler_params=_params(("parallel",), VMEM_SMALL),
        name="gmlp",
    )(p_mix, p_mix, p_mix, ln_w, ln_b, w_s, b_s_t, ys)


def _linear_scan(a, b, h0):
    rows, cols = a.shape
    groups = rows // SUBLANES
    a = a.reshape(groups, SUBLANES, cols)
    b = b.reshape(groups, SUBLANES, cols)
    sub = lax.broadcasted_iota(jnp.int32, a.shape, 1)
    d = 1
    while d < SUBLANES:
        keep = sub >= d
        b = b + a * jnp.where(keep, pltpu.roll(b, d, 1), 0.0)
        a = a * jnp.where(keep, pltpu.roll(a, d, 1), 1.0)
        d *= 2
    out = []
    for j in range(groups):
        hj = b[j] + a[j] * h0
        h0 = hj[SUBLANES - 1:]
        out.append(hj)
    return jnp.concatenate(out, axis=0)


def _rglru_body(x_ref, z_ref, cw_ref, cb_ref, wg_ref, bg_ref, ap_ref, _ys_ref, y_ref, prev_ref, carry_ref):
    step = pl.program_id(0)

    @pl.when(step == 0)
    def _():
        prev_ref[...] = jnp.zeros_like(prev_ref)
        carry_ref[...] = jnp.zeros_like(carry_ref)

    x = x_ref[...]
    tm = x.shape[0]
    xc = _causal_conv(x, prev_ref[...], cw_ref[...]) + cb_ref[...]
    prev_ref[...] = x[tm - SUBLANES:]
    sp = _softplus(-ap_ref[...])
    seq_start = (_row_iota((tm, LRU_BW)) == 0) & (step == 0)
    for n in range(LRU_BLOCKS):
        cols = slice(n * LRU_BW, (n + 1) * LRU_BW)
        xn = xc[:, cols]
        gt = jnp.dot(xn.astype(BF16), wg_ref[n].astype(BF16), preferred_element_type=F32)
        r = _sigmoid(gt[:, :LRU_BW] + bg_ref[:, cols])
        ig = _sigmoid(gt[:, LRU_BW:] + bg_ref[:, BRANCH_W + n * LRU_BW:BRANCH_W + (n + 1) * LRU_BW])
        log_a = -LRU_C * r * sp[:, cols]
        a = jnp.exp(log_a)
        m2 = jnp.tanh(-log_a) * (a * a + 1.0)
        mult = jnp.where(m2 > 0.0, m2 * lax.rsqrt(m2), 0.0)
        mult = jnp.where(seq_start, 1.0, mult)
        hs = _linear_scan(a, mult * ig * xn, carry_ref[:, cols])
        carry_ref[:, cols] = hs[tm - 1:]
        y_ref[:, cols] = (hs * _silu(z_ref[:, cols])).astype(y_ref.dtype)


def _rglru(p_mix, conv_w, conv_b, w_gate, b_gate, a_param, layer, ys):
    seq = p_mix.shape[0]
    tm = TM_LRU
    return pl.pallas_call(
        _rglru_body,
        grid=(seq // tm,),
        in_specs=[_mix_spec(tm, 8), _mix_spec(tm, 9),
                  _layer_spec(layer, conv_w.shape[1:]), _layer_spec(layer, (1, BRANCH_W)),
                  _layer_spec(layer, w_gate.shape[1:]), _layer_spec(layer, (1, 2 * BRANCH_W)),
                  _layer_spec(layer, (1, BRANCH_W)), _YS_ALIAS_SPEC],
        out_specs=_ys_spec(tm, 2),
        out_shape=_ys_shape(seq),
        input_output_aliases={7: 0},
        scratch_shapes=[pltpu.VMEM((SUBLANES, BRANCH_W), F32), pltpu.VMEM((1, BRANCH_W), F32)],
        compiler_params=_params(("arbitrary",), VMEM_SMALL),
        name="rglru",
    )(p_mix, p_mix, conv_w, conv_b, w_gate, b_gate, a_param, ys)


def _sconv_body(b_ref, c_ref, x_ref, z_ref, w_ref, _ys_ref, y_ref, prev_ref):
    @pl.when(pl.program_id(0) == 0)
    def _():
        prev_ref[...] = jnp.zeros_like(prev_ref)

    p = c_ref[...] * x_ref[...]
    conv = _causal_conv(p, prev_ref[...], w_ref[...])
    prev_ref[...] = p[p.shape[0] - SUBLANES:]
    y_ref[...] = (b_ref[...] * conv * _silu(z_ref[...])).astype(y_ref.dtype)


def _sconv(p_mix, w, layer, ys):
    seq = p_mix.shape[0]
    tm = TM_CONV
    return pl.pallas_call(
        _sconv_body,
        grid=(seq // tm,),
        in_specs=[_mix_spec(tm, 10), _mix_spec(tm, 11), _mix_spec(tm, 12), _mix_spec(tm, 13),
                  _layer_spec(layer, w.shape[1:]), _YS_ALIAS_SPEC],
        out_specs=_ys_spec(tm, 3),
        out_shape=_ys_shape(seq),
        input_output_aliases={5: 0},
        scratch_shapes=[pltpu.VMEM((SUBLANES, BRANCH_W), F32)],
        compiler_params=_params(("arbitrary",), VMEM_SMALL),
        name="sconv",
    )(p_mix, p_mix, p_mix, p_mix, w, ys)


def _merge_body(h_ref, y_ref, wg_ref, wb_ref, o_ref, acc_ref):
    g = pl.program_id(2)
    gate = _sigmoid(_dot_nt(h_ref[...], wg_ref[...].astype(BF16)))
    term = gate * jnp.dot(y_ref[...], wb_ref[...].astype(BF16), preferred_element_type=F32)

    @pl.when(g == 0)
    def _():
        acc_ref[...] = term

    @pl.when(g > 0)
    def _():
        acc_ref[...] += term

    @pl.when(g == N_BRANCH - 1)
    def _():
        o_ref[...] = acc_ref[...].astype(o_ref.dtype)


def _merge(h, ys, w_in_t, w_branch, layer):
    seq = h.shape[0]
    n_tiles = D_MODEL // TN_MERGE

    def gate_rows(i, j, g):
        return _aligned_rows((MIX_W + N_GATE_COLS) // SUBLANES + (g * n_tiles + j) * (TN_MERGE // SUBLANES))

    return pl.pallas_call(
        _merge_body,
        grid=(seq // TM_MERGE, n_tiles, N_BRANCH),
        in_specs=[pl.BlockSpec((TM_MERGE, D_MODEL), lambda i, j, g: (i, 0), pipeline_mode=pl.Buffered(1)),
                  pl.BlockSpec((None, TM_MERGE, BRANCH_W), lambda i, j, g: (g, i, 0)),
                  _wt_spec(layer, TN_MERGE, gate_rows),
                  pl.BlockSpec((None, None, BRANCH_W, TN_MERGE), lambda i, j, g: (layer, g, 0, j))],
        out_specs=pl.BlockSpec((TM_MERGE, TN_MERGE), lambda i, j, g: (i, j)),
        out_shape=jax.ShapeDtypeStruct((seq, D_MODEL), BF16),
        scratch_shapes=[pltpu.VMEM((TM_MERGE, TN_MERGE), F32)],
        compiler_params=_params(("parallel", "parallel", "arbitrary"), VMEM_MM),
        name="merge",
    )(h, ys, w_in_t, w_branch)


def kernel(x, pre_w, post_w, w_in, mlstm_gate_bias, mlstm_norm_w, gmlp_ln_w, gmlp_ln_b, gmlp_w_s, gmlp_b_s,
           rglru_conv_w, rglru_conv_b, rglru_w_gate, rglru_b_gate, rglru_a_param, sconv_w, w_branch, w_out):
    batch = x.shape[0]
    depth = w_in.shape[0]
    row = lambda p: p[:, None, :]
    pre_w, post_w = row(pre_w), row(post_w)
    norm_w, ln_w, ln_b = row(mlstm_norm_w), row(gmlp_ln_w), row(gmlp_ln_b)
    conv_b, b_gate, a_param = row(rglru_conv_b), row(rglru_b_gate), row(rglru_a_param)
    gate_bias = jnp.pad(mlstm_gate_bias, ((0, 0), (0, LANES - N_GATE_COLS)))
    b_s_t = jnp.swapaxes(gmlp_b_s, 1, 2)
    w_in_t = jnp.swapaxes(w_in, 1, 2)
    outs = []
    for bi in range(batch):
        xb = x[bi]
        h, gates = _prenorm(xb, pre_w, 0, w_in_t)
        for l in range(depth):
            p_mix = _proj(h, w_in_t, l)
            ys = _mlstm(p_mix, gates, gate_bias[l][None], norm_w, l)
            ys = _gmlp(p_mix, ln_w, ln_b, gmlp_w_s, b_s_t[l], l, ys)
            ys = _rglru(p_mix, rglru_conv_w, conv_b, rglru_w_gate, b_gate, a_param, l, ys)
            ys = _sconv(p_mix, sconv_w, l, ys)
            merged = _merge(h, ys, w_in_t, w_branch, l)
            out = _out_proj(merged, w_out, l)
            if l + 1 < depth:
                xb, h, gates = _post_next(xb, out, post_w, pre_w, l, w_in_t)
            else:
                xb = _post(xb, out, post_w, l)
        outs.append(xb)
    return jnp.stack(outs, axis=0)
```

```python
import jax
import jax.numpy as jnp
from jax import lax
from jax.experimental import pallas as pl
from jax.experimental.pallas import tpu as pltpu

F32 = jnp.float32
BF16 = jnp.bfloat16

D_MODEL = 4096
BRANCH_W = D_MODEL // 4
N_BRANCH = 4
HEADS = 8
HEAD_DIM = BRANCH_W // HEADS
CHUNK = 128
GROUPS = 8
GROUP_W = BRANCH_W // GROUPS
LRU_BLOCKS = 8
LRU_BW = BRANCH_W // LRU_BLOCKS
LRU_C = 8.0
EPS = 1e-6

LANES = 128
SUBLANES = 8
N_GATE_COLS = 2 * HEADS
GATE_COL0 = 5 * BRANCH_W
MIX_W = 14 * BRANCH_W

TM_NORM = 256
TM_PROJ = 2048
TN_PROJ = 512
TM_MERGE = 1024
TN_MERGE = 512
TM_OUT = 2048
TN_OUT = 512
TM_MIX = 2 * CHUNK
VMEM_MM = 52 * 1024 * 1024
VMEM_SMALL = 40 * 1024 * 1024


def _params(semantics, vmem):
    return pltpu.CompilerParams(dimension_semantics=semantics, vmem_limit_bytes=vmem)


def _sigmoid(x):
    return 1.0 / (1.0 + jnp.exp(-x))


def _silu(x):
    return x * _sigmoid(x)


def _softplus(x):
    return jnp.maximum(x, 0.0) + jnp.log1p(jnp.exp(-jnp.abs(x)))


def _row_iota(shape):
    return lax.broadcasted_iota(jnp.int32, shape, 0)


def _lane_iota(shape):
    return lax.broadcasted_iota(jnp.int32, shape, len(shape) - 1)


def _shift_rows(x, d, fill):
    rows, cols = x.shape
    if d % SUBLANES == 0:
        return jnp.concatenate([jnp.full((d, cols), fill, x.dtype), x[:rows - d]], axis=0)
    return jnp.where(_row_iota(x.shape) >= d, pltpu.roll(x, d, 0), fill)


def _causal_conv(x, prev, w):
    taps = w.shape[0]

    def conv(arr):
        y = pltpu.roll(arr, taps - 1, 0) * w[0:1]
        for k in range(1, taps - 1):
            y = y + pltpu.roll(arr, taps - 1 - k, 0) * w[k:k + 1]
        return y + arr * w[taps - 1:taps]

    body = conv(x)
    head = conv(jnp.concatenate([prev, x[:SUBLANES]], axis=0))[SUBLANES:]
    return jnp.concatenate([head, body[SUBLANES:]], axis=0)


def _rms(x):
    return x * lax.rsqrt(jnp.mean(x * x, axis=-1, keepdims=True) + EPS)


def _row_spec(tm, width):
    return pl.BlockSpec((tm, width), lambda i: (i, 0))


def _full_spec(shape):
    return pl.BlockSpec(shape, lambda *_: (0,) * len(shape))


def _layer_spec(layer, shape):
    return pl.BlockSpec((None,) + shape, lambda *_: (layer,) + (0,) * len(shape))


def _dot_nt(a, b_t):
    return lax.dot_general(a, b_t, (((1,), (1,)), ((), ())), preferred_element_type=F32)


def _aligned_rows(octets):
    return octets * SUBLANES


def _wt_spec(layer, rows, row_start):
    return pl.BlockSpec((None, pl.Element(rows), pl.Element(D_MODEL)),
                        lambda *idx: (layer, row_start(*idx), 0))


def _gate_rows(w):
    return jnp.where(_row_iota(w.shape) < N_GATE_COLS, w, 0.0).astype(BF16)


def _norm_proj(x, pw_ref, wg_ref, h_ref, g_ref):
    hb = (_rms(x) * pw_ref[...]).astype(BF16)
    h_ref[...] = hb
    g_ref[...] = _dot_nt(hb, _gate_rows(wg_ref[...]))


def _prenorm_body(x_ref, pw_ref, wg_ref, h_ref, g_ref):
    _norm_proj(x_ref[...], pw_ref, wg_ref, h_ref, g_ref)


def _post_body(x_ref, o_ref, qw_ref, xo_ref):
    xo_ref[...] = x_ref[...] + _rms(o_ref[...]) * qw_ref[...]


def _post_next_body(x_ref, o_ref, qw_ref, pw_ref, wg_ref, xo_ref, h_ref, g_ref):
    xn = x_ref[...] + _rms(o_ref[...]) * qw_ref[...]
    xo_ref[...] = xn
    _norm_proj(xn, pw_ref, wg_ref, h_ref, g_ref)


def _norm_out(seq):
    shapes = (jax.ShapeDtypeStruct((seq, D_MODEL), BF16), jax.ShapeDtypeStruct((seq, LANES), F32))
    specs = (_row_spec(TM_NORM, D_MODEL), _row_spec(TM_NORM, LANES))
    return shapes, specs


def _gate_spec(layer):
    return _wt_spec(layer, LANES, lambda *_: GATE_COL0)


def _prenorm(x, pre_w, layer, w_in_t):
    seq = x.shape[0]
    shapes, specs = _norm_out(seq)
    return pl.pallas_call(
        _prenorm_body,
        grid=(seq // TM_NORM,),
        in_specs=[_row_spec(TM_NORM, D_MODEL), _layer_spec(layer, (1, D_MODEL)), _gate_spec(layer)],
        out_specs=specs, out_shape=shapes,
        compiler_params=_params(("parallel",), VMEM_SMALL),
        name="prenorm",
    )(x, pre_w, w_in_t)


def _post(x, o, post_w, layer):
    seq = x.shape[0]
    return pl.pallas_call(
        _post_body,
        grid=(seq // TM_NORM,),
        in_specs=[_row_spec(TM_NORM, D_MODEL), _row_spec(TM_NORM, D_MODEL), _layer_spec(layer, (1, D_MODEL))],
        out_specs=_row_spec(TM_NORM, D_MODEL),
        out_shape=jax.ShapeDtypeStruct((seq, D_MODEL), F32),
        compiler_params=_params(("parallel",), VMEM_SMALL),
        name="post",
    )(x, o, post_w)


def _post_next(x, o, post_w, pre_w, layer, w_in_t):
    seq = x.shape[0]
    shapes, specs = _norm_out(seq)
    return pl.pallas_call(
        _post_next_body,
        grid=(seq // TM_NORM,),
        in_specs=[_row_spec(TM_NORM, D_MODEL), _row_spec(TM_NORM, D_MODEL), _layer_spec(layer, (1, D_MODEL)),
                  _layer_spec(layer + 1, (1, D_MODEL)), _gate_spec(layer + 1)],
        out_specs=(_row_spec(TM_NORM, D_MODEL),) + specs,
        out_shape=(jax.ShapeDtypeStruct((seq, D_MODEL), F32),) + shapes,
        compiler_params=_params(("parallel",), VMEM_SMALL),
        name="post_next",
    )(x, o, post_w, pre_w, w_in_t)


def _proj_rows(i, j):
    return _aligned_rows(j * (TN_PROJ // SUBLANES)
                         + jnp.where(j >= GATE_COL0 // TN_PROJ, N_GATE_COLS // SUBLANES, 0))


def _proj_body(h_ref, w_ref, o_ref):
    o_ref[...] = _dot_nt(h_ref[...], w_ref[...].astype(BF16))


def _proj(h, w_in_t, layer):
    m, k = h.shape
    return pl.pallas_call(
        _proj_body,
        grid=(m // TM_PROJ, MIX_W // TN_PROJ),
        in_specs=[pl.BlockSpec((TM_PROJ, k), lambda i, j: (i, 0), pipeline_mode=pl.Buffered(1)),
                  _wt_spec(layer, TN_PROJ, _proj_rows)],
        out_specs=pl.BlockSpec((TM_PROJ, TN_PROJ), lambda i, j: (i, j)),
        out_shape=jax.ShapeDtypeStruct((m, MIX_W), F32),
        compiler_params=_params(("parallel", "parallel"), VMEM_MM),
        name="proj",
    )(h, w_in_t)


def _out_body(a_ref, w_ref, o_ref):
    o_ref[...] = jnp.dot(a_ref[...], w_ref[...].astype(BF16), preferred_element_type=F32)


def _out_proj(merged, w_out, layer):
    m, k = merged.shape
    n = w_out.shape[2]
    return pl.pallas_call(
        _out_body,
        grid=(m // TM_OUT, n // TN_OUT),
        in_specs=[pl.BlockSpec((TM_OUT, k), lambda i, j: (i, 0), pipeline_mode=pl.Buffered(1)),
                  pl.BlockSpec((None, k, TN_OUT), lambda i, j: (layer, 0, j))],
        out_specs=pl.BlockSpec((TM_OUT, TN_OUT), lambda i, j: (i, j)),
        out_shape=jax.ShapeDtypeStruct((m, n), F32),
        compiler_params=_params(("parallel", "parallel"), VMEM_MM),
        name="out_proj",
    )(merged, w_out)


def _mlstm_chunk(rows, m_prev, q_ref, k_ref, v_ref, o_ref, z_ref, g_ref, gb_ref, nw_ref, y_ref, state_ref):
    i_pre = g_ref[rows, :] + gb_ref[...]
    f_pre = pltpu.roll(i_pre, LANES - HEADS, 1)
    lf = -_softplus(-f_pre)
    b = lf
    d = 1
    while d < CHUNK:
        b = b + _shift_rows(b, d, 0.0)
        d *= 2
    beta = i_pre - b
    cm = beta
    d = 1
    while d < CHUNK:
        cm = jnp.maximum(cm, _shift_rows(cm, d, -jnp.inf))
        d *= 2
    inter = b + m_prev
    m_t = jnp.maximum(inter, b + cm)
    alpha = b - m_t
    w_inter = jnp.exp(inter - m_t)
    e_negm = jnp.exp(-m_t)
    b_last = b[CHUNK - 1:CHUNK]
    m_new = jnp.maximum(b_last + m_prev, b_last + cm[CHUNK - 1:CHUNK])
    decay = jnp.exp(b_last + m_prev - m_new)
    ws = jnp.exp(b_last + beta - m_new)
    beta_t = beta.T

    shape = (CHUNK, CHUNK)
    tril = _row_iota(shape) >= _lane_iota(shape)
    lane0 = _lane_iota(shape) == 0

    heads = range(HEADS)
    sls = [slice(h * HEAD_DIM, (h + 1) * HEAD_DIM) for h in heads]
    qs = [q_ref[rows, sl].astype(BF16) for sl in sls]
    k_ts = [(k_ref[rows, sl] * (HEAD_DIM ** -0.5)).T.astype(BF16) for sl in sls]
    qks = [jnp.dot(qs[h], k_ts[h], preferred_element_type=F32) for h in heads]
    q_states = [jnp.dot(qs[h], state_ref[h].astype(BF16), preferred_element_type=F32) for h in heads]
    scores = [qks[h] * jnp.exp(jnp.where(tril, alpha[:, h:h + 1] + beta_t[h:h + 1, :], -jnp.inf))
              for h in heads]
    pvs = [jnp.dot(scores[h].astype(BF16), v_ref[rows, sls[h]].astype(BF16), preferred_element_type=F32)
           for h in heads]
    wi_cols = [w_inter[:, h:h + 1] for h in heads]
    dens = [jnp.sum(scores[h], axis=-1, keepdims=True) + wi_cols[h] * q_states[h][:, HEAD_DIM:HEAD_DIM + 1]
            for h in heads]
    r_dens = [1.0 / jnp.maximum(jnp.abs(dens[h]), e_negm[:, h:h + 1]) for h in heads]
    hhs = [(pvs[h] + wi_cols[h] * q_states[h][:, :HEAD_DIM]) * r_dens[h] for h in heads]
    r_norms = [lax.rsqrt(jnp.mean(hh * hh, axis=-1, keepdims=True) + EPS) for hh in hhs]
    for h in heads:
        sl = sls[h]
        y = hhs[h] * r_norms[h] * nw_ref[:, sl]
        y = (y * _sigmoid(o_ref[rows, sl])) * _silu(z_ref[rows, sl])
        y_ref[rows, sl] = y.astype(y_ref.dtype)
    for h in heads:
        ws_col = ws[:, h:h + 1]
        upd = jnp.concatenate([(ws_col * v_ref[rows, sls[h]]).astype(BF16),
                               jnp.where(lane0, ws_col, 0.0).astype(BF16)], axis=1)
        state_ref[h] = decay[:, h:h + 1] * state_ref[h] + jnp.dot(k_ts[h], upd, preferred_element_type=F32)
    return m_new


def _mlstm_mix(q_ref, k_ref, v_ref, o_ref, z_ref, g_ref, gb_ref, nw_ref, y_ref, state_ref, m_ref):
    m = m_ref[...]
    for c in range(q_ref.shape[0] // CHUNK):
        rows = slice(c * CHUNK, (c + 1) * CHUNK)
        m = _mlstm_chunk(rows, m, q_ref, k_ref, v_ref, o_ref, z_ref, g_ref, gb_ref, nw_ref, y_ref, state_ref)
    m_ref[...] = m


def _gmlp_mix(u_ref, v_ref, z_ref, lw_ref, lb_ref, ws_ref, bst_ref, y_ref):
    v = v_ref[...]
    vc = v - jnp.mean(v, axis=-1, keepdims=True)
    var = jnp.mean(vc * vc, axis=-1, keepdims=True)
    vn = (vc * lax.rsqrt(var + EPS) * lw_ref[...] + lb_ref[...]).astype(BF16)
    shape = (CHUNK, CHUNK)
    tril = _row_iota(shape) >= _lane_iota(shape)
    for g in range(GROUPS):
        cols = slice(g * GROUP_W, (g + 1) * GROUP_W)
        w = jnp.where(tril, ws_ref[g], 0.0).astype(BF16)
        bias = bst_ref[:, g:g + 1]
        for c in range(v.shape[0] // CHUNK):
            rows = slice(c * CHUNK, (c + 1) * CHUNK)
            sp = jnp.dot(w, vn[rows, cols], preferred_element_type=F32) + bias
            y = u_ref[rows, cols] * sp * _silu(z_ref[rows, cols])
            y_ref[rows, cols] = y.astype(y_ref.dtype)


def _linear_scan(a, b, h0):
    rows, cols = a.shape
    groups = rows // SUBLANES
    a = a.reshape(groups, SUBLANES, cols)
    b = b.reshape(groups, SUBLANES, cols)
    sub = lax.broadcasted_iota(jnp.int32, a.shape, 1)
    d = 1
    while d < SUBLANES:
        keep = sub >= d
        b = b + a * jnp.where(keep, pltpu.roll(b, d, 1), 0.0)
        a = a * jnp.where(keep, pltpu.roll(a, d, 1), 1.0)
        d *= 2
    out = []
    for j in range(groups):
        hj = b[j] + a[j] * h0
        h0 = hj[SUBLANES - 1:]
        out.append(hj)
    return jnp.concatenate(out, axis=0)


def _rglru_mix(x_ref, z_ref, cw_ref, cb_ref, wg_ref, bg_ref, ap_ref, y_ref, prev_ref, carry_ref):
    step = pl.program_id(0)
    x = x_ref[...]
    tm = x.shape[0]
    xc = _causal_conv(x, prev_ref[...], cw_ref[...]) + cb_ref[...]
    prev_ref[...] = x[tm - SUBLANES:]
    sp = _softplus(-ap_ref[...])
    seq_start = (_row_iota((tm, LRU_BW)) == 0) & (step == 0)
    for n in range(LRU_BLOCKS):
        cols = slice(n * LRU_BW, (n + 1) * LRU_BW)
        xn = xc[:, cols]
        gt = jnp.dot(xn.astype(BF16), wg_ref[n].astype(BF16), preferred_element_type=F32)
        r = _sigmoid(gt[:, :LRU_BW] + bg_ref[:, cols])
        ig = _sigmoid(gt[:, LRU_BW:] + bg_ref[:, BRANCH_W + n * LRU_BW:BRANCH_W + (n + 1) * LRU_BW])
        log_a = -LRU_C * r * sp[:, cols]
        a = jnp.exp(log_a)
        m2 = jnp.tanh(-log_a) * (a * a + 1.0)
        mult = jnp.where(m2 > 0.0, m2 * lax.rsqrt(m2), 0.0)
        mult = jnp.where(seq_start, 1.0, mult)
        hs = _linear_scan(a, mult * ig * xn, carry_ref[:, cols])
        carry_ref[:, cols] = hs[tm - 1:]
        y_ref[:, cols] = (hs * _silu(z_ref[:, cols])).astype(y_ref.dtype)


def _sconv_mix(b_ref, c_ref, x_ref, z_ref, w_ref, y_ref, prev_ref):
    p = c_ref[...] * x_ref[...]
    conv = _causal_conv(p, prev_ref[...], w_ref[...])
    prev_ref[...] = p[p.shape[0] - SUBLANES:]
    y_ref[...] = (b_ref[...] * conv * _silu(z_ref[...])).astype(y_ref.dtype)


def _mixers_body(*refs):
    p = refs[:14]
    (g_ref, gb_ref, nw_ref, lw_ref, lb_ref, ws_ref, bst_ref, cw_ref, cb_ref, wg_ref, bg_ref, ap_ref, sw_ref,
     y_ref, state_ref, m_ref, lru_prev_ref, lru_carry_ref, conv_prev_ref) = refs[14:]

    @pl.when(pl.program_id(0) == 0)
    def _():
        for ref in (state_ref, m_ref, lru_prev_ref, lru_carry_ref, conv_prev_ref):
            ref[...] = jnp.zeros_like(ref)

    _mlstm_mix(*p[0:5], g_ref, gb_ref, nw_ref, y_ref.at[0], state_ref, m_ref)
    _rglru_mix(*p[8:10], cw_ref, cb_ref, wg_ref, bg_ref, ap_ref, y_ref.at[2], lru_prev_ref, lru_carry_ref)
    _gmlp_mix(*p[5:8], lw_ref, lb_ref, ws_ref, bst_ref, y_ref.at[1])
    _sconv_mix(*p[10:14], sw_ref, y_ref.at[3], conv_prev_ref)


def _mixers(p_mix, gates, gate_bias, norm_w, ln_w, ln_b, w_s, b_s_t, conv_w, conv_b, w_gate, b_gate, a_param,
            sconv_w, layer):
    seq = p_mix.shape[0]
    tm = TM_MIX
    mix_specs = [pl.BlockSpec((tm, BRANCH_W), lambda i, c=c: (i, c)) for c in range(MIX_W // BRANCH_W)]
    vec = _layer_spec(layer, (1, BRANCH_W))
    return pl.pallas_call(
        _mixers_body,
        grid=(seq // tm,),
        in_specs=mix_specs + [
            _row_spec(tm, LANES), _full_spec((1, LANES)), vec,
            vec, vec, _layer_spec(layer, (GROUPS, CHUNK, CHUNK)), _full_spec((CHUNK, GROUPS)),
            _layer_spec(layer, conv_w.shape[1:]), vec, _layer_spec(layer, w_gate.shape[1:]),
            _layer_spec(layer, (1, 2 * BRANCH_W)), vec,
            _layer_spec(layer, sconv_w.shape[1:])],
        out_specs=pl.BlockSpec((N_BRANCH, tm, BRANCH_W), lambda i: (0, i, 0)),
        out_shape=jax.ShapeDtypeStruct((N_BRANCH, seq, BRANCH_W), BF16),
        scratch_shapes=[pltpu.VMEM((HEADS, HEAD_DIM, 2 * HEAD_DIM), F32), pltpu.VMEM((1, LANES), F32),
                        pltpu.VMEM((SUBLANES, BRANCH_W), F32), pltpu.VMEM((1, BRANCH_W), F32),
                        pltpu.VMEM((SUBLANES, BRANCH_W), F32)],
        compiler_params=_params(("arbitrary",), VMEM_MM),
        name="mixers",
    )(*([p_mix] * (MIX_W // BRANCH_W)), gates, gate_bias, norm_w, ln_w, ln_b, w_s, b_s_t, conv_w, conv_b,
      w_gate, b_gate, a_param, sconv_w)


def _merge_body(h_ref, y_ref, wg_ref, wb_ref, o_ref, acc_ref):
    g = pl.program_id(2)
    gate = _sigmoid(_dot_nt(h_ref[...], wg_ref[...].astype(BF16)))
    term = gate * jnp.dot(y_ref[...], wb_ref[...].astype(BF16), preferred_element_type=F32)

    @pl.when(g == 0)
    def _():
        acc_ref[...] = term

    @pl.when(g > 0)
    def _():
        acc_ref[...] += term

    @pl.when(g == N_BRANCH - 1)
    def _():
        o_ref[...] = acc_ref[...].astype(o_ref.dtype)


def _merge(h, ys, w_in_t, w_branch, layer):
    seq = h.shape[0]
    n_tiles = D_MODEL // TN_MERGE

    def gate_rows(i, j, g):
        return _aligned_rows((MIX_W + N_GATE_COLS) // SUBLANES + (g * n_tiles + j) * (TN_MERGE // SUBLANES))

    return pl.pallas_call(
        _merge_body,
        grid=(seq // TM_MERGE, n_tiles, N_BRANCH),
        in_specs=[pl.BlockSpec((TM_MERGE, D_MODEL), lambda i, j, g: (i, 0), pipeline_mode=pl.Buffered(1)),
                  pl.BlockSpec((None, TM_MERGE, BRANCH_W), lambda i, j, g: (g, i, 0)),
                  _wt_spec(layer, TN_MERGE, gate_rows),
                  pl.BlockSpec((None, None, BRANCH_W, TN_MERGE), lambda i, j, g: (layer, g, 0, j))],
        out_specs=pl.BlockSpec((TM_MERGE, TN_MERGE), lambda i, j, g: (i, j)),
        out_shape=jax.ShapeDtypeStruct((seq, D_MODEL), BF16),
        scratch_shapes=[pltpu.VMEM((TM_MERGE, TN_MERGE), F32)],
        compiler_params=_params(("parallel", "parallel", "arbitrary"), VMEM_MM),
        name="merge",
    )(h, ys, w_in_t, w_branch)


def kernel(x, pre_w, post_w, w_in, mlstm_gate_bias, mlstm_norm_w, gmlp_ln_w, gmlp_ln_b, gmlp_w_s, gmlp_b_s,
           rglru_conv_w, rglru_conv_b, rglru_w_gate, rglru_b_gate, rglru_a_param, sconv_w, w_branch, w_out):
    batch = x.shape[0]
    depth = w_in.shape[0]
    row = lambda p: p[:, None, :]
    pre_w, post_w = row(pre_w), row(post_w)
    norm_w, ln_w, ln_b = row(mlstm_norm_w), row(gmlp_ln_w), row(gmlp_ln_b)
    conv_b, b_gate, a_param = row(rglru_conv_b), row(rglru_b_gate), row(rglru_a_param)
    gate_bias = jnp.pad(mlstm_gate_bias, ((0, 0), (0, LANES - N_GATE_COLS)))
    b_s_t = jnp.swapaxes(gmlp_b_s, 1, 2)
    w_in_t = jnp.swapaxes(w_in, 1, 2)
    outs = []
    for bi in range(batch):
        xb = x[bi]
        h, gates = _prenorm(xb, pre_w, 0, w_in_t)
        for l in range(depth):
            p_mix = _proj(h, w_in_t, l)
            ys = _mixers(p_mix, gates, gate_bias[l][None], norm_w, ln_w, ln_b, gmlp_w_s, b_s_t[l],
                         rglru_conv_w, conv_b, rglru_w_gate, b_gate, a_param, sconv_w, l)
            merged = _merge(h, ys, w_in_t, w_branch, l)
            out = _out_proj(merged, w_out, l)
            if l + 1 < depth:
                xb, h, gates = _post_next(xb, out, post_w, pre_w, l, w_in_t)
            else:
                xb = _post(xb, out, post_w, l)
        outs.append(xb)
    return jnp.stack(outs, axis=0)
```

```python
import jax
import jax.numpy as jnp
from jax import lax
from jax.experimental import pallas as pl
from jax.experimental.pallas import tpu as pltpu

F32 = jnp.float32
BF16 = jnp.bfloat16

D_MODEL = 4096
BRANCH_W = D_MODEL // 4
N_BRANCH = 4
HEADS = 8
HEAD_DIM = BRANCH_W // HEADS
CHUNK = 128
GROUPS = 8
GROUP_W = BRANCH_W // GROUPS
LRU_BLOCKS = 8
LRU_BW = BRANCH_W // LRU_BLOCKS
LRU_C = 8.0
EPS = 1e-6

LANES = 128
SUBLANES = 8
N_GATE_COLS = 2 * HEADS
GATE_COL0 = 5 * BRANCH_W
MIX_W = 14 * BRANCH_W

TM_NORM = 256
TM_PROJ = 2048
TN_PROJ = 512
TM_MERGE = 1024
TN_MERGE = 512
TM_OUT = 2048
TN_OUT = 512
TM_MIX = 2 * CHUNK
VMEM_MM = 52 * 1024 * 1024
VMEM_SMALL = 40 * 1024 * 1024


def _params(semantics, vmem):
    return pltpu.CompilerParams(dimension_semantics=semantics, vmem_limit_bytes=vmem)


def _sigmoid(x):
    return 1.0 / (1.0 + jnp.exp(-x))


def _silu(x):
    return x * _sigmoid(x)


def _softplus(x):
    return jnp.maximum(x, 0.0) + jnp.log1p(jnp.exp(-jnp.abs(x)))


def _row_iota(shape):
    return lax.broadcasted_iota(jnp.int32, shape, 0)


def _lane_iota(shape):
    return lax.broadcasted_iota(jnp.int32, shape, len(shape) - 1)


def _shift_rows(x, d, fill):
    rows, cols = x.shape
    if d % SUBLANES == 0:
        return jnp.concatenate([jnp.full((d, cols), fill, x.dtype), x[:rows - d]], axis=0)
    return jnp.where(_row_iota(x.shape) >= d, pltpu.roll(x, d, 0), fill)


def _causal_conv(x, prev, w):
    taps = w.shape[0]

    def conv(arr):
        y = pltpu.roll(arr, taps - 1, 0) * w[0:1]
        for k in range(1, taps - 1):
            y = y + pltpu.roll(arr, taps - 1 - k, 0) * w[k:k + 1]
        return y + arr * w[taps - 1:taps]

    body = conv(x)
    head = conv(jnp.concatenate([prev, x[:SUBLANES]], axis=0))[SUBLANES:]
    return jnp.concatenate([head, body[SUBLANES:]], axis=0)


def _rms(x):
    return x * lax.rsqrt(jnp.mean(x * x, axis=-1, keepdims=True) + EPS)


def _row_spec(tm, width):
    return pl.BlockSpec((tm, width), lambda i: (i, 0))


def _full_spec(shape):
    return pl.BlockSpec(shape, lambda *_: (0,) * len(shape))


def _layer_spec(layer, shape):
    return pl.BlockSpec((None,) + shape, lambda *_: (layer,) + (0,) * len(shape))


def _dot_nt(a, b_t):
    return lax.dot_general(a, b_t, (((1,), (1,)), ((), ())), preferred_element_type=F32)


def _aligned_rows(octets):
    return octets * SUBLANES


def _wt_spec(layer, rows, row_start):
    return pl.BlockSpec((None, pl.Element(rows), pl.Element(D_MODEL)),
                        lambda *idx: (layer, row_start(*idx), 0))


def _gate_rows(w):
    return jnp.where(_row_iota(w.shape) < N_GATE_COLS, w, 0.0).astype(BF16)


def _norm_proj(x, pw_ref, wg_ref, h_ref, g_ref):
    hb = (_rms(x) * pw_ref[...]).astype(BF16)
    h_ref[...] = hb
    g_ref[...] = _dot_nt(hb, _gate_rows(wg_ref[...]))


def _prenorm_body(x_ref, pw_ref, wg_ref, h_ref, g_ref):
    _norm_proj(x_ref[...], pw_ref, wg_ref, h_ref, g_ref)


def _post_body(x_ref, o_ref, qw_ref, xo_ref):
    xo_ref[...] = x_ref[...] + _rms(o_ref[...]) * qw_ref[...]


def _post_next_body(x_ref, o_ref, qw_ref, pw_ref, wg_ref, xo_ref, h_ref, g_ref):
    xn = x_ref[...] + _rms(o_ref[...]) * qw_ref[...]
    xo_ref[...] = xn
    _norm_proj(xn, pw_ref, wg_ref, h_ref, g_ref)


def _norm_out(seq):
    shapes = (jax.ShapeDtypeStruct((seq, D_MODEL), BF16), jax.ShapeDtypeStruct((seq, LANES), F32))
    specs = (_row_spec(TM_NORM, D_MODEL), _row_spec(TM_NORM, LANES))
    return shapes, specs


def _gate_spec(layer):
    return _wt_spec(layer, LANES, lambda *_: GATE_COL0)


def _prenorm(x, pre_w, layer, w_in_t):
    seq = x.shape[0]
    shapes, specs = _norm_out(seq)
    return pl.pallas_call(
        _prenorm_body,
        grid=(seq // TM_NORM,),
        in_specs=[_row_spec(TM_NORM, D_MODEL), _layer_spec(layer, (1, D_MODEL)), _gate_spec(layer)],
        out_specs=specs, out_shape=shapes,
        compiler_params=_params(("parallel",), VMEM_SMALL),
        name="prenorm",
    )(x, pre_w, w_in_t)


def _post(x, o, post_w, layer):
    seq = x.shape[0]
    return pl.pallas_call(
        _post_body,
        grid=(seq // TM_NORM,),
        in_specs=[_row_spec(TM_NORM, D_MODEL), _row_spec(TM_NORM, D_MODEL), _layer_spec(layer, (1, D_MODEL))],
        out_specs=_row_spec(TM_NORM, D_MODEL),
        out_shape=jax.ShapeDtypeStruct((seq, D_MODEL), F32),
        compiler_params=_params(("parallel",), VMEM_SMALL),
        name="post",
    )(x, o, post_w)


def _post_next(x, o, post_w, pre_w, layer, w_in_t):
    seq = x.shape[0]
    shapes, specs = _norm_out(seq)
    return pl.pallas_call(
        _post_next_body,
        grid=(seq // TM_NORM,),
        in_specs=[_row_spec(TM_NORM, D_MODEL), _row_spec(TM_NORM, D_MODEL), _layer_spec(layer, (1, D_MODEL)),
                  _layer_spec(layer + 1, (1, D_MODEL)), _gate_spec(layer + 1)],
        out_specs=(_row_spec(TM_NORM, D_MODEL),) + specs,
        out_shape=(jax.ShapeDtypeStruct((seq, D_MODEL), F32),) + shapes,
        compiler_params=_params(("parallel",), VMEM_SMALL),
        name="post_next",
    )(x, o, post_w, pre_w, w_in_t)


def _proj_rows(i, j):
    return _aligned_rows(j * (TN_PROJ // SUBLANES)
                         + jnp.where(j >= GATE_COL0 // TN_PROJ, N_GATE_COLS // SUBLANES, 0))


def _proj_body(h_ref, w_ref, o_ref):
    o_ref[...] = _dot_nt(h_ref[...], w_ref[...].astype(BF16))


def _proj(h, w_in_t, layer):
    m, k = h.shape
    return pl.pallas_call(
        _proj_body,
        grid=(m // TM_PROJ, MIX_W // TN_PROJ),
        in_specs=[pl.BlockSpec((TM_PROJ, k), lambda i, j: (i, 0), pipeline_mode=pl.Buffered(1)),
                  _wt_spec(layer, TN_PROJ, _proj_rows)],
        out_specs=pl.BlockSpec((TM_PROJ, TN_PROJ), lambda i, j: (i, j)),
        out_shape=jax.ShapeDtypeStruct((m, MIX_W), F32),
        compiler_params=_params(("parallel", "parallel"), VMEM_MM),
        name="proj",
    )(h, w_in_t)


def _out_body(a_ref, w_ref, o_ref):
    o_ref[...] = jnp.dot(a_ref[...], w_ref[...].astype(BF16), preferred_element_type=F32)


def _out_proj(merged, w_out, layer):
    m, k = merged.shape
    n = w_out.shape[2]
    return pl.pallas_call(
        _out_body,
        grid=(m // TM_OUT, n // TN_OUT),
        in_specs=[pl.BlockSpec((TM_OUT, k), lambda i, j: (i, 0), pipeline_mode=pl.Buffered(1)),
                  pl.BlockSpec((None, k, TN_OUT), lambda i, j: (layer, 0, j))],
        out_specs=pl.BlockSpec((TM_OUT, TN_OUT), lambda i, j: (i, j)),
        out_shape=jax.ShapeDtypeStruct((m, n), F32),
        compiler_params=_params(("parallel", "parallel"), VMEM_MM),
        name="out_proj",
    )(merged, w_out)


def _mlstm_chunk(rows, m_prev, q_ref, k_ref, v_ref, o_ref, z_ref, g_ref, gb_ref, nw_ref, y_ref, state_ref):
    i_pre = g_ref[rows, :] + gb_ref[...]
    f_pre = pltpu.roll(i_pre, LANES - HEADS, 1)
    lf = -_softplus(-f_pre)
    b = lf
    d = 1
    while d < CHUNK:
        b = b + _shift_rows(b, d, 0.0)
        d *= 2
    beta = i_pre - b
    cm = beta
    d = 1
    while d < CHUNK:
        cm = jnp.maximum(cm, _shift_rows(cm, d, -jnp.inf))
        d *= 2
    inter = b + m_prev
    m_t = jnp.maximum(inter, b + cm)
    alpha = b - m_t
    w_inter = jnp.exp(inter - m_t)
    e_negm = jnp.exp(-m_t)
    b_last = b[CHUNK - 1:CHUNK]
    m_new = jnp.maximum(b_last + m_prev, b_last + cm[CHUNK - 1:CHUNK])
    decay = jnp.exp(b_last + m_prev - m_new)
    ws = jnp.exp(b_last + beta - m_new)
    beta_t = beta.T

    shape = (CHUNK, CHUNK)
    tril = _row_iota(shape) >= _lane_iota(shape)
    lane0 = _lane_iota(shape) == 0

    heads = range(HEADS)
    sls = [slice(h * HEAD_DIM, (h + 1) * HEAD_DIM) for h in heads]
    qs = [q_ref[rows, sl].astype(BF16) for sl in sls]
    k_ts = [(k_ref[rows, sl] * (HEAD_DIM ** -0.5)).T.astype(BF16) for sl in sls]
    qks = [jnp.dot(qs[h], k_ts[h], preferred_element_type=F32) for h in heads]
    q_states = [jnp.dot(qs[h], state_ref[h].astype(BF16), preferred_element_type=F32) for h in heads]
    scores = [qks[h] * jnp.exp(jnp.where(tril, alpha[:, h:h + 1] + beta_t[h:h + 1, :], -jnp.inf))
              for h in heads]
    pvs = [jnp.dot(scores[h].astype(BF16), v_ref[rows, sls[h]].astype(BF16), preferred_element_type=F32)
           for h in heads]
    wi_cols = [w_inter[:, h:h + 1] for h in heads]
    dens = [jnp.sum(scores[h], axis=-1, keepdims=True) + wi_cols[h] * q_states[h][:, HEAD_DIM:HEAD_DIM + 1]
            for h in heads]
    r_dens = [1.0 / jnp.maximum(jnp.abs(dens[h]), e_negm[:, h:h + 1]) for h in heads]
    hhs = [(pvs[h] + wi_cols[h] * q_states[h][:, :HEAD_DIM]) * r_dens[h] for h in heads]
    r_norms = [lax.rsqrt(jnp.mean(hh * hh, axis=-1, keepdims=True) + EPS) for hh in hhs]
    for h in heads:
        sl = sls[h]
        y = hhs[h] * r_norms[h] * nw_ref[:, sl]
        y = (y * _sigmoid(o_ref[rows, sl])) * _silu(z_ref[rows, sl])
        y_ref[rows, sl] = y.astype(y_ref.dtype)
    for h in heads:
        ws_col = ws[:, h:h + 1]
        upd = jnp.concatenate([(ws_col * v_ref[rows, sls[h]]).astype(BF16),
                               jnp.where(lane0, ws_col, 0.0).astype(BF16)], axis=1)
        state_ref[h] = decay[:, h:h + 1] * state_ref[h] + jnp.dot(k_ts[h], upd, preferred_element_type=F32)
    return m_new


def _mlstm_mix(q_ref, k_ref, v_ref, o_ref, z_ref, g_ref, gb_ref, nw_ref, y_ref, state_ref, m_ref):
    m = m_ref[...]
    for c in range(q_ref.shape[0] // CHUNK):
        rows = slice(c * CHUNK, (c + 1) * CHUNK)
        m = _mlstm_chunk(rows, m, q_ref, k_ref, v_ref, o_ref, z_ref, g_ref, gb_ref, nw_ref, y_ref, state_ref)
    m_ref[...] = m


def _gmlp_mix(u_ref, v_ref, z_ref, lw_ref, lb_ref, ws_ref, bst_ref, y_ref):
    v = v_ref[...]
    vc = v - jnp.mean(v, axis=-1, keepdims=True)
    var = jnp.mean(vc * vc, axis=-1, keepdims=True)
    vn = (vc * lax.rsqrt(var + EPS) * lw_ref[...] + lb_ref[...]).astype(BF16)
    shape = (CHUNK, CHUNK)
    tril = _row_iota(shape) >= _lane_iota(shape)
    for g in range(GROUPS):
        cols = slice(g * GROUP_W, (g + 1) * GROUP_W)
        w = jnp.where(tril, ws_ref[g], 0.0).astype(BF16)
        bias = bst_ref[:, g:g + 1]
        for c in range(v.shape[0] // CHUNK):
            rows = slice(c * CHUNK, (c + 1) * CHUNK)
            sp = jnp.dot(w, vn[rows, cols], preferred_element_type=F32) + bias
            y = u_ref[rows, cols] * sp * _silu(z_ref[rows, cols])
            y_ref[rows, cols] = y.astype(y_ref.dtype)


def _linear_scan(a, b, h0):
    rows, cols = a.shape
    groups = rows // SUBLANES
    a = a.reshape(groups, SUBLANES, cols)
    b = b.reshape(groups, SUBLANES, cols)
    sub = lax.broadcasted_iota(jnp.int32, a.shape, 1)
    d = 1
    while d < SUBLANES:
        keep = sub >= d
        b = b + a * jnp.where(keep, pltpu.roll(b, d, 1), 0.0)
        a = a * jnp.where(keep, pltpu.roll(a, d, 1), 1.0)
        d *= 2
    out = []
    for j in range(groups):
        hj = b[j] + a[j] * h0
        h0 = hj[SUBLANES - 1:]
        out.append(hj)
    return jnp.concatenate(out, axis=0)


def _rglru_mix(x_ref, z_ref, cw_ref, cb_ref, wg_ref, bg_ref, ap_ref, y_ref, prev_ref, carry_ref):
    step = pl.program_id(0)
    x = x_ref[...]
    tm = x.shape[0]
    xc = _causal_conv(x, prev_ref[...], cw_ref[...]) + cb_ref[...]
    prev_ref[...] = x[tm - SUBLANES:]
    sp = _softplus(-ap_ref[...])
    seq_start = (_row_iota((tm, LRU_BW)) == 0) & (step == 0)
    for n in range(LRU_BLOCKS):
        cols = slice(n * LRU_BW, (n + 1) * LRU_BW)
        xn = xc[:, cols]
        gt = jnp.dot(xn.astype(BF16), wg_ref[n].astype(BF16), preferred_element_type=F32)
        r = _sigmoid(gt[:, :LRU_BW] + bg_ref[:, cols])
        ig = _sigmoid(gt[:, LRU_BW:] + bg_ref[:, BRANCH_W + n * LRU_BW:BRANCH_W + (n + 1) * LRU_BW])
        log_a = -LRU_C * r * sp[:, cols]
        a = jnp.exp(log_a)
        m2 = jnp.tanh(-log_a) * (a * a + 1.0)
        mult = jnp.where(m2 > 0.0, m2 * lax.rsqrt(m2), 0.0)
        mult = jnp.where(seq_start, 1.0, mult)
        hs = _linear_scan(a, mult * ig * xn, carry_ref[:, cols])
        carry_ref[:, cols] = hs[tm - 1:]
        y_ref[:, cols] = (hs * _silu(z_ref[:, cols])).astype(y_ref.dtype)


def _sconv_mix(b_ref, c_ref, x_ref, z_ref, w_ref, y_ref, prev_ref):
    p = c_ref[...] * x_ref[...]
    conv = _causal_conv(p, prev_ref[...], w_ref[...])
    prev_ref[...] = p[p.shape[0] - SUBLANES:]
    y_ref[...] = (b_ref[...] * conv * _silu(z_ref[...])).astype(y_ref.dtype)


def _mixers_body(*refs):
    p = refs[:14]
    (g_ref, gb_ref, nw_ref, lw_ref, lb_ref, ws_ref, bst_ref, cw_ref, cb_ref, wg_ref, bg_ref, ap_ref, sw_ref,
     y_ref, state_ref, m_ref, lru_prev_ref, lru_carry_ref, conv_prev_ref) = refs[14:]

    @pl.when(pl.program_id(0) == 0)
    def _():
        for ref in (state_ref, m_ref, lru_prev_ref, lru_carry_ref, conv_prev_ref):
            ref[...] = jnp.zeros_like(ref)

    _mlstm_mix(*p[0:5], g_ref, gb_ref, nw_ref, y_ref.at[0], state_ref, m_ref)
    _rglru_mix(*p[8:10], cw_ref, cb_ref, wg_ref, bg_ref, ap_ref, y_ref.at[2], lru_prev_ref, lru_carry_ref)
    _gmlp_mix(*p[5:8], lw_ref, lb_ref, ws_ref, bst_ref, y_ref.at[1])
    _sconv_mix(*p[10:14], sw_ref, y_ref.at[3], conv_prev_ref)


def _mixers(p_mix, gates, gate_bias, norm_w, ln_w, ln_b, w_s, b_s_t, conv_w, conv_b, w_gate, b_gate, a_param,
            sconv_w, layer):
    seq = p_mix.shape[0]
    tm = TM_MIX
    mix_specs = [pl.BlockSpec((tm, BRANCH_W), lambda i, c=c: (i, c)) for c in range(MIX_W // BRANCH_W)]
    vec = _layer_spec(layer, (1, BRANCH_W))
    return pl.pallas_call(
        _mixers_body,
        grid=(seq // tm,),
        in_specs=mix_specs + [
            _row_spec(tm, LANES), _full_spec((1, LANES)), vec,
            vec, vec, _layer_spec(layer, (GROUPS, CHUNK, CHUNK)), _full_spec((CHUNK, GROUPS)),
            _layer_spec(layer, conv_w.shape[1:]), vec, _layer_spec(layer, w_gate.shape[1:]),
            _layer_spec(layer, (1, 2 * BRANCH_W)), vec,
            _layer_spec(layer, sconv_w.shape[1:])],
        out_specs=pl.BlockSpec((N_BRANCH, tm, BRANCH_W), lambda i: (0, i, 0)),
        out_shape=jax.ShapeDtypeStruct((N_BRANCH, seq, BRANCH_W), BF16),
        scratch_shapes=[pltpu.VMEM((HEADS, HEAD_DIM, 2 * HEAD_DIM), F32), pltpu.VMEM((1, LANES), F32),
                        pltpu.VMEM((SUBLANES, BRANCH_W), F32), pltpu.VMEM((1, BRANCH_W), F32),
                        pltpu.VMEM((SUBLANES, BRANCH_W), F32)],
        compiler_params=_params(("arbitrary",), VMEM_MM),
        name="mixers",
    )(*([p_mix] * (MIX_W // BRANCH_W)), gates, gate_bias, norm_w, ln_w, ln_b, w_s, b_s_t, conv_w, conv_b,
      w_gate, b_gate, a_param, sconv_w)


def _merge_body(h_ref, y_ref, wg_ref, wb_ref, o_ref, pre_ref, acc_ref):
    s = pl.program_id(0)

    @pl.when(s == 0)
    def _():
        pre_ref[...] = jnp.zeros_like(pre_ref)
        acc_ref[...] = jnp.zeros_like(acc_ref)

    g_prev = lax.rem(jnp.maximum(s - 1, 0), N_BRANCH)
    gate = _sigmoid(pre_ref[lax.rem(s + 1, 2)])
    term = gate * jnp.dot(y_ref[...], wb_ref[...].astype(BF16), preferred_element_type=F32)
    acc = term + jnp.where(g_prev == 0, 0.0, acc_ref[...])
    acc_ref[...] = acc

    pre_ref[lax.rem(s, 2)] = _dot_nt(h_ref[...], wg_ref[...].astype(BF16))

    @pl.when(g_prev == N_BRANCH - 1)
    def _():
        o_ref[...] = acc.astype(o_ref.dtype)


def _merge(h, ys, w_in_t, w_branch, layer):
    seq = h.shape[0]
    n_tiles = D_MODEL // TN_MERGE
    per_row_tile = n_tiles * N_BRANCH
    n_items = (seq // TM_MERGE) * per_row_tile

    def item(s):
        return s // per_row_tile, lax.rem(s // N_BRANCH, n_tiles), lax.rem(s, N_BRANCH)

    def cur(s):
        return item(jnp.minimum(s, n_items - 1))

    def prev(s):
        return item(jnp.maximum(s - 1, 0))

    def gate_rows(s):
        _, j, g = cur(s)
        return _aligned_rows((MIX_W + N_GATE_COLS) // SUBLANES + (g * n_tiles + j) * (TN_MERGE // SUBLANES))

    return pl.pallas_call(
        _merge_body,
        grid=(n_items + 1,),
        in_specs=[pl.BlockSpec((TM_MERGE, D_MODEL), lambda s: (cur(s)[0], 0), pipeline_mode=pl.Buffered(1)),
                  pl.BlockSpec((None, TM_MERGE, BRANCH_W), lambda s: (prev(s)[2], prev(s)[0], 0)),
                  _wt_spec(layer, TN_MERGE, gate_rows),
                  pl.BlockSpec((None, None, BRANCH_W, TN_MERGE), lambda s: (layer, prev(s)[2], 0, prev(s)[1]))],
        out_specs=pl.BlockSpec((TM_MERGE, TN_MERGE), lambda s: (prev(s)[0], prev(s)[1])),
        out_shape=jax.ShapeDtypeStruct((seq, D_MODEL), BF16),
        scratch_shapes=[pltpu.VMEM((2, TM_MERGE, TN_MERGE), F32), pltpu.VMEM((TM_MERGE, TN_MERGE), F32)],
        compiler_params=_params(("arbitrary",), VMEM_MM),
        name="merge",
    )(h, ys, w_in_t, w_branch)


def kernel(x, pre_w, post_w, w_in, mlstm_gate_bias, mlstm_norm_w, gmlp_ln_w, gmlp_ln_b, gmlp_w_s, gmlp_b_s,
           rglru_conv_w, rglru_conv_b, rglru_w_gate, rglru_b_gate, rglru_a_param, sconv_w, w_branch, w_out):
    batch = x.shape[0]
    depth = w_in.shape[0]
    row = lambda p: p[:, None, :]
    pre_w, post_w = row(pre_w), row(post_w)
    norm_w, ln_w, ln_b = row(mlstm_norm_w), row(gmlp_ln_w), row(gmlp_ln_b)
    conv_b, b_gate, a_param = row(rglru_conv_b), row(rglru_b_gate), row(rglru_a_param)
    gate_bias = jnp.pad(mlstm_gate_bias, ((0, 0), (0, LANES - N_GATE_COLS)))
    b_s_t = jnp.swapaxes(gmlp_b_s, 1, 2)
    w_in_t = jnp.swapaxes(w_in, 1, 2)
    outs = []
    for bi in range(batch):
        xb = x[bi]
        h, gates = _prenorm(xb, pre_w, 0, w_in_t)
        for l in range(depth):
            p_mix = _proj(h, w_in_t, l)
            ys = _mixers(p_mix, gates, gate_bias[l][None], norm_w, ln_w, ln_b, gmlp_w_s, b_s_t[l],
                         rglru_conv_w, conv_b, rglru_w_gate, b_gate, a_param, sconv_w, l)
            merged = _merge(h, ys, w_in_t, w_branch, l)
            out = _out_proj(merged, w_out, l)
            if l + 1 < depth:
                xb, h, gates = _post_next(xb, out, post_w, pre_w, l, w_in_t)
            else:
                xb = _post(xb, out, post_w, l)
        outs.append(xb)
    return jnp.stack(outs, axis=0)
```

```python
import jax
import jax.numpy as jnp
from jax import lax
from jax.experimental import pallas as pl
from jax.experimental.pallas import tpu as pltpu

F32 = jnp.float32
BF16 = jnp.bfloat16

D_MODEL = 4096
BRANCH_W = D_MODEL // 4
N_BRANCH = 4
HEADS = 8
HEAD_DIM = BRANCH_W // HEADS
CHUNK = 128
GROUPS = 8
GROUP_W = BRANCH_W // GROUPS
LRU_BLOCKS = 8
LRU_BW = BRANCH_W // LRU_BLOCKS
LRU_C = 8.0
EPS = 1e-6

LANES = 128
SUBLANES = 8
N_GATE_COLS = 2 * HEADS
GATE_COL0 = 5 * BRANCH_W
MIX_W = 14 * BRANCH_W

TM_NORM = 256
TM_PROJ = 2048
TN_PROJ = 512
TM_MERGE = 1024
TN_MERGE = 512
TM_OUT = 2048
TN_OUT = 512
TM_MIX = 2 * CHUNK
VMEM_MM = 52 * 1024 * 1024
VMEM_SMALL = 40 * 1024 * 1024


def _params(semantics, vmem):
    return pltpu.CompilerParams(dimension_semantics=semantics, vmem_limit_bytes=vmem)


def _sigmoid(x):
    return 1.0 / (1.0 + jnp.exp(-x))


def _silu(x):
    return x * _sigmoid(x)


def _softplus(x):
    return jnp.maximum(x, 0.0) + jnp.log1p(jnp.exp(-jnp.abs(x)))


def _row_iota(shape):
    return lax.broadcasted_iota(jnp.int32, shape, 0)


def _lane_iota(shape):
    return lax.broadcasted_iota(jnp.int32, shape, len(shape) - 1)


def _shift_rows(x, d, fill):
    rows, cols = x.shape
    if d % SUBLANES == 0:
        return jnp.concatenate([jnp.full((d, cols), fill, x.dtype), x[:rows - d]], axis=0)
    return jnp.where(_row_iota(x.shape) >= d, pltpu.roll(x, d, 0), fill)


def _causal_conv(x, prev, w):
    taps = w.shape[0]

    def conv(arr):
        y = pltpu.roll(arr, taps - 1, 0) * w[0:1]
        for k in range(1, taps - 1):
            y = y + pltpu.roll(arr, taps - 1 - k, 0) * w[k:k + 1]
        return y + arr * w[taps - 1:taps]

    body = conv(x)
    head = conv(jnp.concatenate([prev, x[:SUBLANES]], axis=0))[SUBLANES:]
    return jnp.concatenate([head, body[SUBLANES:]], axis=0)


def _rms(x):
    return x * lax.rsqrt(jnp.mean(x * x, axis=-1, keepdims=True) + EPS)


def _row_spec(tm, width):
    return pl.BlockSpec((tm, width), lambda i: (i, 0))


def _full_spec(shape):
    return pl.BlockSpec(shape, lambda *_: (0,) * len(shape))


def _layer_spec(layer, shape):
    return pl.BlockSpec((None,) + shape, lambda *_: (layer,) + (0,) * len(shape))


def _dot_nt(a, b_t):
    return lax.dot_general(a, b_t, (((1,), (1,)), ((), ())), preferred_element_type=F32)


def _aligned_rows(octets):
    return octets * SUBLANES


def _wt_spec(layer, rows, row_start):
    return pl.BlockSpec((None, pl.Element(rows), pl.Element(D_MODEL)),
                        lambda *idx: (layer, row_start(*idx), 0))


def _gate_rows(w):
    return jnp.where(_row_iota(w.shape) < N_GATE_COLS, w, 0.0).astype(BF16)


def _norm_proj(x, pw_ref, wg_ref, h_ref, g_ref):
    hb = (_rms(x) * pw_ref[...]).astype(BF16)
    h_ref[...] = hb
    g_ref[...] = _dot_nt(hb, _gate_rows(wg_ref[...]))


def _prenorm_body(x_ref, pw_ref, wg_ref, h_ref, g_ref):
    _norm_proj(x_ref[...], pw_ref, wg_ref, h_ref, g_ref)


def _post_body(x_ref, o_ref, qw_ref, xo_ref):
    xo_ref[...] = x_ref[...] + _rms(o_ref[...]) * qw_ref[...]


def _post_next_body(x_ref, o_ref, qw_ref, pw_ref, wg_ref, xo_ref, h_ref, g_ref):
    xn = x_ref[...] + _rms(o_ref[...]) * qw_ref[...]
    xo_ref[...] = xn
    _norm_proj(xn, pw_ref, wg_ref, h_ref, g_ref)


def _norm_out(seq):
    shapes = (jax.ShapeDtypeStruct((seq, D_MODEL), BF16), jax.ShapeDtypeStruct((seq, LANES), F32))
    specs = (_row_spec(TM_NORM, D_MODEL), _row_spec(TM_NORM, LANES))
    return shapes, specs


def _gate_spec(layer):
    return _wt_spec(layer, LANES, lambda *_: GATE_COL0)


def _prenorm(x, pre_w, layer, w_in_t):
    seq = x.shape[0]
    shapes, specs = _norm_out(seq)
    return pl.pallas_call(
        _prenorm_body,
        grid=(seq // TM_NORM,),
        in_specs=[_row_spec(TM_NORM, D_MODEL), _layer_spec(layer, (1, D_MODEL)), _gate_spec(layer)],
        out_specs=specs, out_shape=shapes,
        compiler_params=_params(("parallel",), VMEM_SMALL),
        name="prenorm",
    )(x, pre_w, w_in_t)


def _post(x, o, post_w, layer):
    seq = x.shape[0]
    return pl.pallas_call(
        _post_body,
        grid=(seq // TM_NORM,),
        in_specs=[_row_spec(TM_NORM, D_MODEL), _row_spec(TM_NORM, D_MODEL), _layer_spec(layer, (1, D_MODEL))],
        out_specs=_row_spec(TM_NORM, D_MODEL),
        out_shape=jax.ShapeDtypeStruct((seq, D_MODEL), F32),
        compiler_params=_params(("parallel",), VMEM_SMALL),
        name="post",
    )(x, o, post_w)


def _post_next(x, o, post_w, pre_w, layer, w_in_t):
    seq = x.shape[0]
    shapes, specs = _norm_out(seq)
    return pl.pallas_call(
        _post_next_body,
        grid=(seq // TM_NORM,),
        in_specs=[_row_spec(TM_NORM, D_MODEL), _row_spec(TM_NORM, D_MODEL), _layer_spec(layer, (1, D_MODEL)),
                  _layer_spec(layer + 1, (1, D_MODEL)), _gate_spec(layer + 1)],
        out_specs=(_row_spec(TM_NORM, D_MODEL),) + specs,
        out_shape=(jax.ShapeDtypeStruct((seq, D_MODEL), F32),) + shapes,
        compiler_params=_params(("parallel",), VMEM_SMALL),
        name="post_next",
    )(x, o, post_w, pre_w, w_in_t)


def _proj_rows(i, j):
    return _aligned_rows(j * (TN_PROJ // SUBLANES)
                         + jnp.where(j >= GATE_COL0 // TN_PROJ, N_GATE_COLS // SUBLANES, 0))


def _proj_body(h_ref, w_ref, o_ref):
    o_ref[...] = _dot_nt(h_ref[...], w_ref[...].astype(BF16))


def _proj(h, w_in_t, layer):
    m, k = h.shape
    return pl.pallas_call(
        _proj_body,
        grid=(m // TM_PROJ, MIX_W // TN_PROJ),
        in_specs=[pl.BlockSpec((TM_PROJ, k), lambda i, j: (i, 0), pipeline_mode=pl.Buffered(1)),
                  _wt_spec(layer, TN_PROJ, _proj_rows)],
        out_specs=pl.BlockSpec((TM_PROJ, TN_PROJ), lambda i, j: (i, j)),
        out_shape=jax.ShapeDtypeStruct((m, MIX_W), F32),
        compiler_params=_params(("parallel", "parallel"), VMEM_MM),
        name="proj",
    )(h, w_in_t)


def _out_body(a_ref, w_ref, o_ref):
    o_ref[...] = jnp.dot(a_ref[...], w_ref[...].astype(BF16), preferred_element_type=F32)


def _out_proj(merged, w_out, layer):
    m, k = merged.shape
    n = w_out.shape[2]
    return pl.pallas_call(
        _out_body,
        grid=(m // TM_OUT, n // TN_OUT),
        in_specs=[pl.BlockSpec((TM_OUT, k), lambda i, j: (i, 0), pipeline_mode=pl.Buffered(1)),
                  pl.BlockSpec((None, k, TN_OUT), lambda i, j: (layer, 0, j))],
        out_specs=pl.BlockSpec((TM_OUT, TN_OUT), lambda i, j: (i, j)),
        out_shape=jax.ShapeDtypeStruct((m, n), F32),
        compiler_params=_params(("parallel", "parallel"), VMEM_MM),
        name="out_proj",
    )(merged, w_out)


def _mlstm_chunk(rows, m_prev, q_ref, k_ref, v_ref, o_ref, z_ref, g_ref, gb_ref, nw_ref, y_ref, state_ref):
    i_pre = g_ref[rows, :] + gb_ref[...]
    f_pre = pltpu.roll(i_pre, LANES - HEADS, 1)
    lf = -_softplus(-f_pre)
    b = lf
    d = 1
    while d < CHUNK:
        b = b + _shift_rows(b, d, 0.0)
        d *= 2
    beta = i_pre - b
    cm = beta
    d = 1
    while d < CHUNK:
        cm = jnp.maximum(cm, _shift_rows(cm, d, -jnp.inf))
        d *= 2
    inter = b + m_prev
    m_t = jnp.maximum(inter, b + cm)
    alpha = b - m_t
    w_inter = jnp.exp(inter - m_t)
    e_negm = jnp.exp(-m_t)
    b_last = b[CHUNK - 1:CHUNK]
    m_new = jnp.maximum(b_last + m_prev, b_last + cm[CHUNK - 1:CHUNK])
    decay = jnp.exp(b_last + m_prev - m_new)
    ws = jnp.exp(b_last + beta - m_new)
    beta_t = beta.T

    shape = (CHUNK, CHUNK)
    tril = _row_iota(shape) >= _lane_iota(shape)
    lane0 = _lane_iota(shape) == 0
    yield

    heads = range(HEADS)
    sls = [slice(h * HEAD_DIM, (h + 1) * HEAD_DIM) for h in heads]
    qs = [q_ref[rows, sl].astype(BF16) for sl in sls]
    k_ts = [(k_ref[rows, sl] * (HEAD_DIM ** -0.5)).T.astype(BF16) for sl in sls]
    qks = [jnp.dot(qs[h], k_ts[h], preferred_element_type=F32) for h in heads]
    q_states = [jnp.dot(qs[h], state_ref[h].astype(BF16), preferred_element_type=F32) for h in heads]
    scores = [qks[h] * jnp.exp(jnp.where(tril, alpha[:, h:h + 1] + beta_t[h:h + 1, :], -jnp.inf))
              for h in heads]
    pvs = [jnp.dot(scores[h].astype(BF16), v_ref[rows, sls[h]].astype(BF16), preferred_element_type=F32)
           for h in heads]
    yield
    wi_cols = [w_inter[:, h:h + 1] for h in heads]
    dens = [jnp.sum(scores[h], axis=-1, keepdims=True) + wi_cols[h] * q_states[h][:, HEAD_DIM:HEAD_DIM + 1]
            for h in heads]
    r_dens = [1.0 / jnp.maximum(jnp.abs(dens[h]), e_negm[:, h:h + 1]) for h in heads]
    hhs = [(pvs[h] + wi_cols[h] * q_states[h][:, :HEAD_DIM]) * r_dens[h] for h in heads]
    r_norms = [lax.rsqrt(jnp.mean(hh * hh, axis=-1, keepdims=True) + EPS) for hh in hhs]
    for h in heads:
        sl = sls[h]
        y = hhs[h] * r_norms[h] * nw_ref[:, sl]
        y = (y * _sigmoid(o_ref[rows, sl])) * _silu(z_ref[rows, sl])
        y_ref[rows, sl] = y.astype(y_ref.dtype)
    yield
    for h in heads:
        ws_col = ws[:, h:h + 1]
        upd = jnp.concatenate([(ws_col * v_ref[rows, sls[h]]).astype(BF16),
                               jnp.where(lane0, ws_col, 0.0).astype(BF16)], axis=1)
        state_ref[h] = decay[:, h:h + 1] * state_ref[h] + jnp.dot(k_ts[h], upd, preferred_element_type=F32)
    yield
    return m_new


def _mlstm_mix(q_ref, k_ref, v_ref, o_ref, z_ref, g_ref, gb_ref, nw_ref, y_ref, state_ref, m_ref):
    m = m_ref[...]
    for c in range(q_ref.shape[0] // CHUNK):
        rows = slice(c * CHUNK, (c + 1) * CHUNK)
        m = yield from _mlstm_chunk(rows, m, q_ref, k_ref, v_ref, o_ref, z_ref, g_ref, gb_ref, nw_ref, y_ref,
                                    state_ref)
    m_ref[...] = m


def _gmlp_mix(u_ref, v_ref, z_ref, lw_ref, lb_ref, ws_ref, bst_ref, y_ref):
    v = v_ref[...]
    vc = v - jnp.mean(v, axis=-1, keepdims=True)
    var = jnp.mean(vc * vc, axis=-1, keepdims=True)
    vn = (vc * lax.rsqrt(var + EPS) * lw_ref[...] + lb_ref[...]).astype(BF16)
    shape = (CHUNK, CHUNK)
    tril = _row_iota(shape) >= _lane_iota(shape)
    yield
    for g in range(GROUPS):
        cols = slice(g * GROUP_W, (g + 1) * GROUP_W)
        w = jnp.where(tril, ws_ref[g], 0.0).astype(BF16)
        bias = bst_ref[:, g:g + 1]
        for c in range(v.shape[0] // CHUNK):
            rows = slice(c * CHUNK, (c + 1) * CHUNK)
            sp = jnp.dot(w, vn[rows, cols], preferred_element_type=F32) + bias
            y = u_ref[rows, cols] * sp * _silu(z_ref[rows, cols])
            y_ref[rows, cols] = y.astype(y_ref.dtype)
        yield


def _linear_scan(a, b, h0):
    rows, cols = a.shape
    groups = rows // SUBLANES
    a = a.reshape(groups, SUBLANES, cols)
    b = b.reshape(groups, SUBLANES, cols)
    sub = lax.broadcasted_iota(jnp.int32, a.shape, 1)
    d = 1
    while d < SUBLANES:
        keep = sub >= d
        b = b + a * jnp.where(keep, pltpu.roll(b, d, 1), 0.0)
        a = a * jnp.where(keep, pltpu.roll(a, d, 1), 1.0)
        d *= 2
    out = []
    for j in range(groups):
        hj = b[j] + a[j] * h0
        h0 = hj[SUBLANES - 1:]
        out.append(hj)
    return jnp.concatenate(out, axis=0)


def _rglru_mix(x_ref, z_ref, cw_ref, cb_ref, wg_ref, bg_ref, ap_ref, y_ref, prev_ref, carry_ref):
    step = pl.program_id(0)
    x = x_ref[...]
    tm = x.shape[0]
    xc = _causal_conv(x, prev_ref[...], cw_ref[...]) + cb_ref[...]
    prev_ref[...] = x[tm - SUBLANES:]
    sp = _softplus(-ap_ref[...])
    seq_start = (_row_iota((tm, LRU_BW)) == 0) & (step == 0)
    yield
    for n in range(LRU_BLOCKS):
        cols = slice(n * LRU_BW, (n + 1) * LRU_BW)
        xn = xc[:, cols]
        gt = jnp.dot(xn.astype(BF16), wg_ref[n].astype(BF16), preferred_element_type=F32)
        r = _sigmoid(gt[:, :LRU_BW] + bg_ref[:, cols])
        ig = _sigmoid(gt[:, LRU_BW:] + bg_ref[:, BRANCH_W + n * LRU_BW:BRANCH_W + (n + 1) * LRU_BW])
        log_a = -LRU_C * r * sp[:, cols]
        a = jnp.exp(log_a)
        m2 = jnp.tanh(-log_a) * (a * a + 1.0)
        mult = jnp.where(m2 > 0.0, m2 * lax.rsqrt(m2), 0.0)
        mult = jnp.where(seq_start, 1.0, mult)
        hs = _linear_scan(a, mult * ig * xn, carry_ref[:, cols])
        carry_ref[:, cols] = hs[tm - 1:]
        y_ref[:, cols] = (hs * _silu(z_ref[:, cols])).astype(y_ref.dtype)
        yield


def _sconv_mix(b_ref, c_ref, x_ref, z_ref, w_ref, y_ref, prev_ref):
    p = c_ref[...] * x_ref[...]
    conv = _causal_conv(p, prev_ref[...], w_ref[...])
    prev_ref[...] = p[p.shape[0] - SUBLANES:]
    yield
    y_ref[...] = (b_ref[...] * conv * _silu(z_ref[...])).astype(y_ref.dtype)


def _alternate(*stage_generators):
    pending = list(stage_generators)
    while pending:
        for gen in list(pending):
            try:
                next(gen)
            except StopIteration:
                pending.remove(gen)


def _mixers_body(*refs):
    p = refs[:14]
    (g_ref, gb_ref, nw_ref, lw_ref, lb_ref, ws_ref, bst_ref, cw_ref, cb_ref, wg_ref, bg_ref, ap_ref, sw_ref,
     y_ref, state_ref, m_ref, lru_prev_ref, lru_carry_ref, conv_prev_ref) = refs[14:]

    @pl.when(pl.program_id(0) == 0)
    def _():
        for ref in (state_ref, m_ref, lru_prev_ref, lru_carry_ref, conv_prev_ref):
            ref[...] = jnp.zeros_like(ref)

    _alternate(_mlstm_mix(*p[0:5], g_ref, gb_ref, nw_ref, y_ref.at[0], state_ref, m_ref),
               _rglru_mix(*p[8:10], cw_ref, cb_ref, wg_ref, bg_ref, ap_ref, y_ref.at[2], lru_prev_ref,
                          lru_carry_ref),
               _gmlp_mix(*p[5:8], lw_ref, lb_ref, ws_ref, bst_ref, y_ref.at[1]),
               _sconv_mix(*p[10:14], sw_ref, y_ref.at[3], conv_prev_ref))


def _mixers(p_mix, gates, gate_bias, norm_w, ln_w, ln_b, w_s, b_s_t, conv_w, conv_b, w_gate, b_gate, a_param,
            sconv_w, layer):
    seq = p_mix.shape[0]
    tm = TM_MIX
    mix_specs = [pl.BlockSpec((tm, BRANCH_W), lambda i, c=c: (i, c)) for c in range(MIX_W // BRANCH_W)]
    vec = _layer_spec(layer, (1, BRANCH_W))
    return pl.pallas_call(
        _mixers_body,
        grid=(seq // tm,),
        in_specs=mix_specs + [
            _row_spec(tm, LANES), _full_spec((1, LANES)), vec,
            vec, vec, _layer_spec(layer, (GROUPS, CHUNK, CHUNK)), _full_spec((CHUNK, GROUPS)),
            _layer_spec(layer, conv_w.shape[1:]), vec, _layer_spec(layer, w_gate.shape[1:]),
            _layer_spec(layer, (1, 2 * BRANCH_W)), vec,
            _layer_spec(layer, sconv_w.shape[1:])],
        out_specs=pl.BlockSpec((N_BRANCH, tm, BRANCH_W), lambda i: (0, i, 0)),
        out_shape=jax.ShapeDtypeStruct((N_BRANCH, seq, BRANCH_W), BF16),
        scratch_shapes=[pltpu.VMEM((HEADS, HEAD_DIM, 2 * HEAD_DIM), F32), pltpu.VMEM((1, LANES), F32),
                        pltpu.VMEM((SUBLANES, BRANCH_W), F32), pltpu.VMEM((1, BRANCH_W), F32),
                        pltpu.VMEM((SUBLANES, BRANCH_W), F32)],
        compiler_params=_params(("arbitrary",), VMEM_MM),
        name="mixers",
    )(*([p_mix] * (MIX_W // BRANCH_W)), gates, gate_bias, norm_w, ln_w, ln_b, w_s, b_s_t, conv_w, conv_b,
      w_gate, b_gate, a_param, sconv_w)


def _merge_body(h_ref, y_ref, wg_ref, wb_ref, o_ref, pre_ref, acc_ref):
    s = pl.program_id(0)

    @pl.when(s == 0)
    def _():
        pre_ref[...] = jnp.zeros_like(pre_ref)
        acc_ref[...] = jnp.zeros_like(acc_ref)

    g_prev = lax.rem(jnp.maximum(s - 1, 0), N_BRANCH)
    gate = _sigmoid(pre_ref[lax.rem(s + 1, 2)])
    term = gate * jnp.dot(y_ref[...], wb_ref[...].astype(BF16), preferred_element_type=F32)
    acc = term + jnp.where(g_prev == 0, 0.0, acc_ref[...])
    acc_ref[...] = acc

    pre_ref[lax.rem(s, 2)] = _dot_nt(h_ref[...], wg_ref[...].astype(BF16))

    @pl.when(g_prev == N_BRANCH - 1)
    def _():
        o_ref[...] = acc.astype(o_ref.dtype)


def _merge(h, ys, w_in_t, w_branch, layer):
    seq = h.shape[0]
    n_tiles = D_MODEL // TN_MERGE
    per_row_tile = n_tiles * N_BRANCH
    n_items = (seq // TM_MERGE) * per_row_tile

    def item(s):
        return s // per_row_tile, lax.rem(s // N_BRANCH, n_tiles), lax.rem(s, N_BRANCH)

    def cur(s):
        return item(jnp.minimum(s, n_items - 1))

    def prev(s):
        return item(jnp.maximum(s - 1, 0))

    def gate_rows(s):
        _, j, g = cur(s)
        return _aligned_rows((MIX_W + N_GATE_COLS) // SUBLANES + (g * n_tiles + j) * (TN_MERGE // SUBLANES))

    return pl.pallas_call(
        _merge_body,
        grid=(n_items + 1,),
        in_specs=[pl.BlockSpec((TM_MERGE, D_MODEL), lambda s: (cur(s)[0], 0), pipeline_mode=pl.Buffered(1)),
                  pl.BlockSpec((None, TM_MERGE, BRANCH_W), lambda s: (prev(s)[2], prev(s)[0], 0)),
                  _wt_spec(layer, TN_MERGE, gate_rows),
                  pl.BlockSpec((None, None, BRANCH_W, TN_MERGE), lambda s: (layer, prev(s)[2], 0, prev(s)[1]))],
        out_specs=pl.BlockSpec((TM_MERGE, TN_MERGE), lambda s: (prev(s)[0], prev(s)[1])),
        out_shape=jax.ShapeDtypeStruct((seq, D_MODEL), BF16),
        scratch_shapes=[pltpu.VMEM((2, TM_MERGE, TN_MERGE), F32), pltpu.VMEM((TM_MERGE, TN_MERGE), F32)],
        compiler_params=_params(("arbitrary",), VMEM_MM),
        name="merge",
    )(h, ys, w_in_t, w_branch)


def kernel(x, pre_w, post_w, w_in, mlstm_gate_bias, mlstm_norm_w, gmlp_ln_w, gmlp_ln_b, gmlp_w_s, gmlp_b_s,
           rglru_conv_w, rglru_conv_b, rglru_w_gate, rglru_b_gate, rglru_a_param, sconv_w, w_branch, w_out):
    batch = x.shape[0]
    depth = w_in.shape[0]
    row = lambda p: p[:, None, :]
    pre_w, post_w = row(pre_w), row(post_w)
    norm_w, ln_w, ln_b = row(mlstm_norm_w), row(gmlp_ln_w), row(gmlp_ln_b)
    conv_b, b_gate, a_param = row(rglru_conv_b), row(rglru_b_gate), row(rglru_a_param)
    gate_bias = jnp.pad(mlstm_gate_bias, ((0, 0), (0, LANES - N_GATE_COLS)))
    b_s_t = jnp.swapaxes(gmlp_b_s, 1, 2)
    w_in_t = jnp.swapaxes(w_in, 1, 2)
    outs = []
    for bi in range(batch):
        xb = x[bi]
        h, gates = _prenorm(xb, pre_w, 0, w_in_t)
        for l in range(depth):
            p_mix = _proj(h, w_in_t, l)
            ys = _mixers(p_mix, gates, gate_bias[l][None], norm_w, ln_w, ln_b, gmlp_w_s, b_s_t[l],
                         rglru_conv_w, conv_b, rglru_w_gate, b_gate, a_param, sconv_w, l)
            merged = _merge(h, ys, w_in_t, w_branch, l)
            out = _out_proj(merged, w_out, l)
            if l + 1 < depth:
                xb, h, gates = _post_next(xb, out, post_w, pre_w, l, w_in_t)
            else:
                xb = _post(xb, out, post_w, l)
        outs.append(xb)
    return jnp.stack(outs, axis=0)
```

```python
import jax
import jax.numpy as jnp
from jax import lax
from jax.experimental import pallas as pl
from jax.experimental.pallas import tpu as pltpu

F32 = jnp.float32
BF16 = jnp.bfloat16

D_MODEL = 4096
BRANCH_W = D_MODEL // 4
N_BRANCH = 4
HEADS = 8
HEAD_DIM = BRANCH_W // HEADS
CHUNK = 128
GROUPS = 8
GROUP_W = BRANCH_W // GROUPS
LRU_BLOCKS = 8
LRU_BW = BRANCH_W // LRU_BLOCKS
LRU_C = 8.0
EPS = 1e-6

LANES = 128
SUBLANES = 8
N_GATE_COLS = 2 * HEADS
GATE_COL0 = 5 * BRANCH_W
MIX_W = 14 * BRANCH_W

TM_NORM = 256
TM_PROJ = 2048
TN_PROJ = 512
TM_MERGE = 1024
TN_MERGE = 512
TM_OUT = 2048
TN_OUT = 512
TM_MIX = 2 * CHUNK
HEAD_GROUP = 8
VMEM_MM = 52 * 1024 * 1024
VMEM_SMALL = 40 * 1024 * 1024


def _params(semantics, vmem):
    return pltpu.CompilerParams(dimension_semantics=semantics, vmem_limit_bytes=vmem)


def _sigmoid(x):
    return 1.0 / (1.0 + jnp.exp(-x))


def _silu(x):
    return x * _sigmoid(x)


def _softplus(x):
    return jnp.maximum(x, 0.0) + jnp.log1p(jnp.exp(-jnp.abs(x)))


def _row_iota(shape):
    return lax.broadcasted_iota(jnp.int32, shape, 0)


def _lane_iota(shape):
    return lax.broadcasted_iota(jnp.int32, shape, len(shape) - 1)


def _shift_rows(x, d, fill):
    rows, cols = x.shape
    if d % SUBLANES == 0:
        return jnp.concatenate([jnp.full((d, cols), fill, x.dtype), x[:rows - d]], axis=0)
    return jnp.where(_row_iota(x.shape) >= d, pltpu.roll(x, d, 0), fill)


def _causal_conv(x, prev, w):
    taps = w.shape[0]

    def conv(arr):
        y = pltpu.roll(arr, taps - 1, 0) * w[0:1]
        for k in range(1, taps - 1):
            y = y + pltpu.roll(arr, taps - 1 - k, 0) * w[k:k + 1]
        return y + arr * w[taps - 1:taps]

    body = conv(x)
    head = conv(jnp.concatenate([prev, x[:SUBLANES]], axis=0))[SUBLANES:]
    return jnp.concatenate([head, body[SUBLANES:]], axis=0)


def _rms(x):
    return x * lax.rsqrt(jnp.mean(x * x, axis=-1, keepdims=True) + EPS)


def _row_spec(tm, width):
    return pl.BlockSpec((tm, width), lambda i: (i, 0))


def _full_spec(shape):
    return pl.BlockSpec(shape, lambda *_: (0,) * len(shape))


def _layer_spec(layer, shape):
    return pl.BlockSpec((None,) + shape, lambda *_: (layer,) + (0,) * len(shape))


def _dot_nt(a, b_t):
    return lax.dot_general(a, b_t, (((1,), (1,)), ((), ())), preferred_element_type=F32)


def _aligned_rows(octets):
    return octets * SUBLANES


def _wt_spec(layer, rows, row_start):
    return pl.BlockSpec((None, pl.Element(rows), pl.Element(D_MODEL)),
                        lambda *idx: (layer, row_start(*idx), 0))


def _gate_rows(w):
    return jnp.where(_row_iota(w.shape) < N_GATE_COLS, w, 0.0).astype(BF16)


def _norm_proj(x, pw_ref, wg_ref, h_ref, g_ref):
    hb = (_rms(x) * pw_ref[...]).astype(BF16)
    h_ref[...] = hb
    g_ref[...] = _dot_nt(hb, _gate_rows(wg_ref[...]))


def _prenorm_body(x_ref, pw_ref, wg_ref, h_ref, g_ref):
    _norm_proj(x_ref[...], pw_ref, wg_ref, h_ref, g_ref)


def _post_body(x_ref, o_ref, qw_ref, xo_ref):
    xo_ref[...] = x_ref[...] + _rms(o_ref[...]) * qw_ref[...]


def _post_next_body(x_ref, o_ref, qw_ref, pw_ref, wg_ref, xo_ref, h_ref, g_ref):
    xn = x_ref[...] + _rms(o_ref[...]) * qw_ref[...]
    xo_ref[...] = xn
    _norm_proj(xn, pw_ref, wg_ref, h_ref, g_ref)


def _norm_out(seq):
    shapes = (jax.ShapeDtypeStruct((seq, D_MODEL), BF16), jax.ShapeDtypeStruct((seq, LANES), F32))
    specs = (_row_spec(TM_NORM, D_MODEL), _row_spec(TM_NORM, LANES))
    return shapes, specs


def _gate_spec(layer):
    return _wt_spec(layer, LANES, lambda *_: GATE_COL0)


def _prenorm(x, pre_w, layer, w_in_t):
    seq = x.shape[0]
    shapes, specs = _norm_out(seq)
    return pl.pallas_call(
        _prenorm_body,
        grid=(seq // TM_NORM,),
        in_specs=[_row_spec(TM_NORM, D_MODEL), _layer_spec(layer, (1, D_MODEL)), _gate_spec(layer)],
        out_specs=specs, out_shape=shapes,
        compiler_params=_params(("parallel",), VMEM_SMALL),
        name="prenorm",
    )(x, pre_w, w_in_t)


def _post(x, o, post_w, layer):
    seq = x.shape[0]
    return pl.pallas_call(
        _post_body,
        grid=(seq // TM_NORM,),
        in_specs=[_row_spec(TM_NORM, D_MODEL), _row_spec(TM_NORM, D_MODEL), _layer_spec(layer, (1, D_MODEL))],
        out_specs=_row_spec(TM_NORM, D_MODEL),
        out_shape=jax.ShapeDtypeStruct((seq, D_MODEL), F32),
        compiler_params=_params(("parallel",), VMEM_SMALL),
        name="post",
    )(x, o, post_w)


def _post_next(x, o, post_w, pre_w, layer, w_in_t):
    seq = x.shape[0]
    shapes, specs = _norm_out(seq)
    return pl.pallas_call(
        _post_next_body,
        grid=(seq // TM_NORM,),
        in_specs=[_row_spec(TM_NORM, D_MODEL), _row_spec(TM_NORM, D_MODEL), _layer_spec(layer, (1, D_MODEL)),
                  _layer_spec(layer + 1, (1, D_MODEL)), _gate_spec(layer + 1)],
        out_specs=(_row_spec(TM_NORM, D_MODEL),) + specs,
        out_shape=(jax.ShapeDtypeStruct((seq, D_MODEL), F32),) + shapes,
        compiler_params=_params(("parallel",), VMEM_SMALL),
        name="post_next",
    )(x, o, post_w, pre_w, w_in_t)


def _proj_rows(i, j):
    return _aligned_rows(j * (TN_PROJ // SUBLANES)
                         + jnp.where(j >= GATE_COL0 // TN_PROJ, N_GATE_COLS // SUBLANES, 0))


def _proj_body(h_ref, w_ref, o_ref):
    o_ref[...] = _dot_nt(h_ref[...], w_ref[...].astype(BF16))


def _proj(h, w_in_t, layer):
    m, k = h.shape
    return pl.pallas_call(
        _proj_body,
        grid=(m // TM_PROJ, MIX_W // TN_PROJ),
        in_specs=[pl.BlockSpec((TM_PROJ, k), lambda i, j: (i, 0), pipeline_mode=pl.Buffered(1)),
                  _wt_spec(layer, TN_PROJ, _proj_rows)],
        out_specs=pl.BlockSpec((TM_PROJ, TN_PROJ), lambda i, j: (i, j)),
        out_shape=jax.ShapeDtypeStruct((m, MIX_W), F32),
        compiler_params=_params(("parallel", "parallel"), VMEM_MM),
        name="proj",
    )(h, w_in_t)


def _out_body(a_ref, w_ref, o_ref):
    o_ref[...] = jnp.dot(a_ref[...], w_ref[...].astype(BF16), preferred_element_type=F32)


def _out_proj(merged, w_out, layer):
    m, k = merged.shape
    n = w_out.shape[2]
    return pl.pallas_call(
        _out_body,
        grid=(m // TM_OUT, n // TN_OUT),
        in_specs=[pl.BlockSpec((TM_OUT, k), lambda i, j: (i, 0), pipeline_mode=pl.Buffered(1)),
                  pl.BlockSpec((None, k, TN_OUT), lambda i, j: (layer, 0, j))],
        out_specs=pl.BlockSpec((TM_OUT, TN_OUT), lambda i, j: (i, j)),
        out_shape=jax.ShapeDtypeStruct((m, n), F32),
        compiler_params=_params(("parallel", "parallel"), VMEM_MM),
        name="out_proj",
    )(merged, w_out)


def _mlstm_chunk(rows, m_prev, q_ref, k_ref, v_ref, o_ref, z_ref, g_ref, gb_ref, nw_ref, y_ref, state_ref):
    i_pre = g_ref[rows, :] + gb_ref[...]
    f_pre = pltpu.roll(i_pre, LANES - HEADS, 1)
    lf = -_softplus(-f_pre)
    b = lf
    d = 1
    while d < CHUNK:
        b = b + _shift_rows(b, d, 0.0)
        d *= 2
    beta = i_pre - b
    cm = beta
    d = 1
    while d < CHUNK:
        cm = jnp.maximum(cm, _shift_rows(cm, d, -jnp.inf))
        d *= 2
    inter = b + m_prev
    m_t = jnp.maximum(inter, b + cm)
    alpha = b - m_t
    w_inter = jnp.exp(inter - m_t)
    e_negm = jnp.exp(-m_t)
    b_last = b[CHUNK - 1:CHUNK]
    m_new = jnp.maximum(b_last + m_prev, b_last + cm[CHUNK - 1:CHUNK])
    decay = jnp.exp(b_last + m_prev - m_new)
    ws = jnp.exp(b_last + beta - m_new)
    beta_t = beta.T

    shape = (CHUNK, CHUNK)
    tril = _row_iota(shape) >= _lane_iota(shape)
    lane0 = _lane_iota(shape) == 0
    yield

    sls = [slice(h * HEAD_DIM, (h + 1) * HEAD_DIM) for h in range(HEADS)]
    for first in range(0, HEADS, HEAD_GROUP):
        heads = range(first, first + HEAD_GROUP)
        qs = {h: q_ref[rows, sls[h]].astype(BF16) for h in heads}
        k_ts = {h: (k_ref[rows, sls[h]] * (HEAD_DIM ** -0.5)).T.astype(BF16) for h in heads}
        qks = {h: jnp.dot(qs[h], k_ts[h], preferred_element_type=F32) for h in heads}
        q_states = {h: jnp.dot(qs[h], state_ref[h].astype(BF16), preferred_element_type=F32)
                    for h in heads}
        scores = {h: qks[h] * jnp.exp(jnp.where(tril, alpha[:, h:h + 1] + beta_t[h:h + 1, :], -jnp.inf))
                  for h in heads}
        pvs = {h: jnp.dot(scores[h].astype(BF16), v_ref[rows, sls[h]].astype(BF16), preferred_element_type=F32)
               for h in heads}
        yield
        wi_cols = {h: w_inter[:, h:h + 1] for h in heads}
        dens = {h: jnp.sum(scores[h], axis=-1, keepdims=True)
                + wi_cols[h] * q_states[h][:, HEAD_DIM:HEAD_DIM + 1] for h in heads}
        r_dens = {h: 1.0 / jnp.maximum(jnp.abs(dens[h]), e_negm[:, h:h + 1]) for h in heads}
        hhs = {h: (pvs[h] + wi_cols[h] * q_states[h][:, :HEAD_DIM]) * r_dens[h] for h in heads}
        r_norms = {h: lax.rsqrt(jnp.mean(hhs[h] * hhs[h], axis=-1, keepdims=True) + EPS) for h in heads}
        for h in heads:
            sl = sls[h]
            y = hhs[h] * r_norms[h] * nw_ref[:, sl]
            y = (y * _sigmoid(o_ref[rows, sl])) * _silu(z_ref[rows, sl])
            y_ref[rows, sl] = y.astype(y_ref.dtype)
        yield
        for h in heads:
            ws_col = ws[:, h:h + 1]
            upd = jnp.concatenate([(ws_col * v_ref[rows, sls[h]]).astype(BF16),
                                   jnp.where(lane0, ws_col, 0.0).astype(BF16)], axis=1)
            state_ref[h] = decay[:, h:h + 1] * state_ref[h] + jnp.dot(k_ts[h], upd, preferred_element_type=F32)
        yield
    return m_new


def _mlstm_mix(q_ref, k_ref, v_ref, o_ref, z_ref, g_ref, gb_ref, nw_ref, y_ref, state_ref, m_ref):
    m = m_ref[...]
    for c in range(q_ref.shape[0] // CHUNK):
        rows = slice(c * CHUNK, (c + 1) * CHUNK)
        m = yield from _mlstm_chunk(rows, m, q_ref, k_ref, v_ref, o_ref, z_ref, g_ref, gb_ref, nw_ref, y_ref,
                                    state_ref)
    m_ref[...] = m


def _gmlp_mix(u_ref, v_ref, z_ref, lw_ref, lb_ref, ws_ref, bst_ref, y_ref):
    v = v_ref[...]
    mean = jnp.mean(v, axis=-1, keepdims=True)
    vc = v - mean
    rstd = lax.rsqrt(jnp.mean(vc * vc, axis=-1, keepdims=True) + EPS)
    shape = (CHUNK, CHUNK)
    tril = _row_iota(shape) >= _lane_iota(shape)
    yield
    for g in range(GROUPS):
        cols = slice(g * GROUP_W, (g + 1) * GROUP_W)
        w = jnp.where(tril, ws_ref[g], 0.0).astype(BF16)
        bias = bst_ref[:, g:g + 1]
        vn = ((v_ref[:, cols] - mean) * rstd * lw_ref[:, cols] + lb_ref[:, cols]).astype(BF16)
        for c in range(v.shape[0] // CHUNK):
            rows = slice(c * CHUNK, (c + 1) * CHUNK)
            sp = jnp.dot(w, vn[rows], preferred_element_type=F32) + bias
            y = u_ref[rows, cols] * sp * _silu(z_ref[rows, cols])
            y_ref[rows, cols] = y.astype(y_ref.dtype)
        yield


def _linear_scan(a, b, h0):
    rows, cols = a.shape
    groups = rows // SUBLANES
    a = a.reshape(groups, SUBLANES, cols)
    b = b.reshape(groups, SUBLANES, cols)
    sub = lax.broadcasted_iota(jnp.int32, a.shape, 1)
    d = 1
    while d < SUBLANES:
        keep = sub >= d
        b = b + a * jnp.where(keep, pltpu.roll(b, d, 1), 0.0)
        a = a * jnp.where(keep, pltpu.roll(a, d, 1), 1.0)
        d *= 2
    out = []
    for j in range(groups):
        hj = b[j] + a[j] * h0
        h0 = hj[SUBLANES - 1:]
        out.append(hj)
    return jnp.concatenate(out, axis=0)


def _rglru_mix(x_ref, z_ref, cw_ref, cb_ref, wg_ref, bg_ref, ap_ref, y_ref, prev_ref, carry_ref):
    step = pl.program_id(0)
    tm = x_ref.shape[0]
    sp = _softplus(-ap_ref[...])
    seq_start = (_row_iota((tm, LRU_BW)) == 0) & (step == 0)
    yield
    for n in range(LRU_BLOCKS):
        cols = slice(n * LRU_BW, (n + 1) * LRU_BW)
        x = x_ref[:, cols]
        xn = _causal_conv(x, prev_ref[:, cols], cw_ref[:, cols]) + cb_ref[:, cols]
        prev_ref[:, cols] = x[tm - SUBLANES:]
        gt = jnp.dot(xn.astype(BF16), wg_ref[n].astype(BF16), preferred_element_type=F32)
        r = _sigmoid(gt[:, :LRU_BW] + bg_ref[:, cols])
        ig = _sigmoid(gt[:, LRU_BW:] + bg_ref[:, BRANCH_W + n * LRU_BW:BRANCH_W + (n + 1) * LRU_BW])
        log_a = -LRU_C * r * sp[:, cols]
        a = jnp.exp(log_a)
        m2 = jnp.tanh(-log_a) * (a * a + 1.0)
        mult = jnp.where(m2 > 0.0, m2 * lax.rsqrt(m2), 0.0)
        mult = jnp.where(seq_start, 1.0, mult)
        hs = _linear_scan(a, mult * ig * xn, carry_ref[:, cols])
        carry_ref[:, cols] = hs[tm - 1:]
        y_ref[:, cols] = (hs * _silu(z_ref[:, cols])).astype(y_ref.dtype)
        yield


def _sconv_mix(b_ref, c_ref, x_ref, z_ref, w_ref, y_ref, prev_ref):
    tm = x_ref.shape[0]
    for n in range(BRANCH_W // LANES):
        cols = slice(n * LANES, (n + 1) * LANES)
        p = c_ref[:, cols] * x_ref[:, cols]
        conv = _causal_conv(p, prev_ref[:, cols], w_ref[:, cols])
        prev_ref[:, cols] = p[tm - SUBLANES:]
        y_ref[:, cols] = (b_ref[:, cols] * conv * _silu(z_ref[:, cols])).astype(y_ref.dtype)
        yield


def _alternate(*stage_generators):
    pending = list(stage_generators)
    while pending:
        for gen in list(pending):
            try:
                next(gen)
            except StopIteration:
                pending.remove(gen)


def _mixers_body(*refs):
    p = refs[:14]
    (g_ref, gb_ref, nw_ref, lw_ref, lb_ref, ws_ref, bst_ref, cw_ref, cb_ref, wg_ref, bg_ref, ap_ref, sw_ref,
     y_ref, state_ref, m_ref, lru_prev_ref, lru_carry_ref, conv_prev_ref) = refs[14:]

    @pl.when(pl.program_id(0) == 0)
    def _():
        for ref in (state_ref, m_ref, lru_prev_ref, lru_carry_ref, conv_prev_ref):
            ref[...] = jnp.zeros_like(ref)

    _alternate(_mlstm_mix(*p[0:5], g_ref, gb_ref, nw_ref, y_ref.at[0], state_ref, m_ref),
               _rglru_mix(*p[8:10], cw_ref, cb_ref, wg_ref, bg_ref, ap_ref, y_ref.at[2], lru_prev_ref,
                          lru_carry_ref),
               _gmlp_mix(*p[5:8], lw_ref, lb_ref, ws_ref, bst_ref, y_ref.at[1]),
               _sconv_mix(*p[10:14], sw_ref, y_ref.at[3], conv_prev_ref))


def _mixers(p_mix, gates, gate_bias, norm_w, ln_w, ln_b, w_s, b_s_t, conv_w, conv_b, w_gate, b_gate, a_param,
            sconv_w, layer):
    seq = p_mix.shape[0]
    tm = TM_MIX
    mix_specs = [pl.BlockSpec((tm, BRANCH_W), lambda i, c=c: (i, c)) for c in range(MIX_W // BRANCH_W)]
    vec = _layer_spec(layer, (1, BRANCH_W))
    return pl.pallas_call(
        _mixers_body,
        grid=(seq // tm,),
        in_specs=mix_specs + [
            _row_spec(tm, LANES), _full_spec((1, LANES)), vec,
            vec, vec, _layer_spec(layer, (GROUPS, CHUNK, CHUNK)), _full_spec((CHUNK, GROUPS)),
            _layer_spec(layer, conv_w.shape[1:]), vec, _layer_spec(layer, w_gate.shape[1:]),
            _layer_spec(layer, (1, 2 * BRANCH_W)), vec,
            _layer_spec(layer, sconv_w.shape[1:])],
        out_specs=pl.BlockSpec((N_BRANCH, tm, BRANCH_W), lambda i: (0, i, 0)),
        out_shape=jax.ShapeDtypeStruct((N_BRANCH, seq, BRANCH_W), BF16),
        scratch_shapes=[pltpu.VMEM((HEADS, HEAD_DIM, 2 * HEAD_DIM), F32), pltpu.VMEM((1, LANES), F32),
                        pltpu.VMEM((SUBLANES, BRANCH_W), F32), pltpu.VMEM((1, BRANCH_W), F32),
                        pltpu.VMEM((SUBLANES, BRANCH_W), F32)],
        compiler_params=_params(("arbitrary",), VMEM_MM),
        name="mixers",
    )(*([p_mix] * (MIX_W // BRANCH_W)), gates, gate_bias, norm_w, ln_w, ln_b, w_s, b_s_t, conv_w, conv_b,
      w_gate, b_gate, a_param, sconv_w)


def _merge_body(h_ref, y_ref, wg_ref, wb_ref, o_ref, pre_ref, acc_ref):
    s = pl.program_id(0)

    @pl.when(s == 0)
    def _():
        pre_ref[...] = jnp.zeros_like(pre_ref)
        acc_ref[...] = jnp.zeros_like(acc_ref)

    g_prev = lax.rem(jnp.maximum(s - 1, 0), N_BRANCH)
    gate = _sigmoid(pre_ref[lax.rem(s + 1, 2)])
    term = gate * jnp.dot(y_ref[...], wb_ref[...].astype(BF16), preferred_element_type=F32)
    acc = term + jnp.where(g_prev == 0, 0.0, acc_ref[...])
    acc_ref[...] = acc

    pre_ref[lax.rem(s, 2)] = _dot_nt(h_ref[...], wg_ref[...].astype(BF16))

    @pl.when(g_prev == N_BRANCH - 1)
    def _():
        o_ref[...] = acc.astype(o_ref.dtype)


def _merge(h, ys, w_in_t, w_branch, layer):
    seq = h.shape[0]
    n_tiles = D_MODEL // TN_MERGE
    per_row_tile = n_tiles * N_BRANCH
    n_items = (seq // TM_MERGE) * per_row_tile

    def item(s):
        return s // per_row_tile, lax.rem(s // N_BRANCH, n_tiles), lax.rem(s, N_BRANCH)

    def cur(s):
        return item(jnp.minimum(s, n_items - 1))

    def prev(s):
        return item(jnp.maximum(s - 1, 0))

    def gate_rows(s):
        _, j, g = cur(s)
        return _aligned_rows((MIX_W + N_GATE_COLS) // SUBLANES + (g * n_tiles + j) * (TN_MERGE // SUBLANES))

    return pl.pallas_call(
        _merge_body,
        grid=(n_items + 1,),
        in_specs=[pl.BlockSpec((TM_MERGE, D_MODEL), lambda s: (cur(s)[0], 0)),
                  pl.BlockSpec((None, TM_MERGE, BRANCH_W), lambda s: (prev(s)[2], prev(s)[0], 0)),
                  _wt_spec(layer, TN_MERGE, gate_rows),
                  pl.BlockSpec((None, None, BRANCH_W, TN_MERGE), lambda s: (layer, prev(s)[2], 0, prev(s)[1]))],
        out_specs=pl.BlockSpec((TM_MERGE, TN_MERGE), lambda s: (prev(s)[0], prev(s)[1])),
        out_shape=jax.ShapeDtypeStruct((seq, D_MODEL), BF16),
        scratch_shapes=[pltpu.VMEM((2, TM_MERGE, TN_MERGE), F32), pltpu.VMEM((TM_MERGE, TN_MERGE), F32)],
        compiler_params=_params(("arbitrary",), VMEM_MM),
        name="merge",
    )(h, ys, w_in_t, w_branch)


def kernel(x, pre_w, post_w, w_in, mlstm_gate_bias, mlstm_norm_w, gmlp_ln_w, gmlp_ln_b, gmlp_w_s, gmlp_b_s,
           rglru_conv_w, rglru_conv_b, rglru_w_gate, rglru_b_gate, rglru_a_param, sconv_w, w_branch, w_out):
    batch = x.shape[0]
    depth = w_in.shape[0]
    row = lambda p: p[:, None, :]
    pre_w, post_w = row(pre_w), row(post_w)
    norm_w, ln_w, ln_b = row(mlstm_norm_w), row(gmlp_ln_w), row(gmlp_ln_b)
    conv_b, b_gate, a_param = row(rglru_conv_b), row(rglru_b_gate), row(rglru_a_param)
    gate_bias = jnp.pad(mlstm_gate_bias, ((0, 0), (0, LANES - N_GATE_COLS)))
    b_s_t = jnp.swapaxes(gmlp_b_s, 1, 2)
    w_in_t = jnp.swapaxes(w_in, 1, 2)
    outs = []
    for bi in range(batch):
        xb = x[bi]
        h, gates = _prenorm(xb, pre_w, 0, w_in_t)
        for l in range(depth):
            p_mix = _proj(h, w_in_t, l)
            ys = _mixers(p_mix, gates, gate_bias[l][None], norm_w, ln_w, ln_b, gmlp_w_s, b_s_t[l],
                         rglru_conv_w, conv_b, rglru_w_gate, b_gate, a_param, sconv_w, l)
            merged = _merge(h, ys, w_in_t, w_branch, l)
            out = _out_proj(merged, w_out, l)
            if l + 1 < depth:
                xb, h, gates = _post_next(xb, out, post_w, pre_w, l, w_in_t)
            else:
                xb = _post(xb, out, post_w, l)
        outs.append(xb)
    return jnp.stack(outs, axis=0)
```

```python
import jax
import jax.numpy as jnp
from jax import lax
from jax.experimental import pallas as pl
from jax.experimental.pallas import tpu as pltpu

F32 = jnp.float32
BF16 = jnp.bfloat16

D_MODEL = 4096
BRANCH_W = D_MODEL // 4
N_BRANCH = 4
HEADS = 8
HEAD_DIM = BRANCH_W // HEADS
CHUNK = 128
GROUPS = 8
GROUP_W = BRANCH_W // GROUPS
LRU_BLOCKS = 8
LRU_BW = BRANCH_W // LRU_BLOCKS
LRU_C = 8.0
EPS = 1e-6

LANES = 128
SUBLANES = 8
N_GATE_COLS = 2 * HEADS
GATE_COL0 = 5 * BRANCH_W
MIX_W = 14 * BRANCH_W

TM_NORM = 256
TM_PROJ = 2048
TN_PROJ = 512
TM_MERGE = 1024
TN_MERGE = 512
TM_OUT = 1024
TN_OUT = 512
TM_MIX = 2 * CHUNK
HEAD_GROUP = 8
VMEM_MM = 52 * 1024 * 1024
VMEM_SMALL = 40 * 1024 * 1024


def _params(semantics, vmem):
    return pltpu.CompilerParams(dimension_semantics=semantics, vmem_limit_bytes=vmem)


def _sigmoid(x):
    return 1.0 / (1.0 + jnp.exp(-x))


def _silu(x):
    return x * _sigmoid(x)


def _softplus(x):
    return jnp.maximum(x, 0.0) + jnp.log1p(jnp.exp(-jnp.abs(x)))


def _row_iota(shape):
    return lax.broadcasted_iota(jnp.int32, shape, 0)


def _lane_iota(shape):
    return lax.broadcasted_iota(jnp.int32, shape, len(shape) - 1)


def _shift_rows(x, d, fill):
    rows, cols = x.shape
    if d % SUBLANES == 0:
        return jnp.concatenate([jnp.full((d, cols), fill, x.dtype), x[:rows - d]], axis=0)
    return jnp.where(_row_iota(x.shape) >= d, pltpu.roll(x, d, 0), fill)


def _causal_conv(x, prev, w):
    taps = w.shape[0]

    def conv(arr):
        y = pltpu.roll(arr, taps - 1, 0) * w[0:1]
        for k in range(1, taps - 1):
            y = y + pltpu.roll(arr, taps - 1 - k, 0) * w[k:k + 1]
        return y + arr * w[taps - 1:taps]

    body = conv(x)
    head = conv(jnp.concatenate([prev, x[:SUBLANES]], axis=0))[SUBLANES:]
    return jnp.concatenate([head, body[SUBLANES:]], axis=0)


def _rms(x):
    return x * lax.rsqrt(jnp.mean(x * x, axis=-1, keepdims=True) + EPS)


def _row_spec(tm, width):
    return pl.BlockSpec((tm, width), lambda i: (i, 0))


def _full_spec(shape):
    return pl.BlockSpec(shape, lambda *_: (0,) * len(shape))


def _layer_spec(layer, shape):
    return pl.BlockSpec((None,) + shape, lambda *_: (layer,) + (0,) * len(shape))


def _dot_nt(a, b_t):
    return lax.dot_general(a, b_t, (((1,), (1,)), ((), ())), preferred_element_type=F32)


def _aligned_rows(octets):
    return octets * SUBLANES


def _wt_spec(layer, rows, row_start):
    return pl.BlockSpec((None, pl.Element(rows), pl.Element(D_MODEL)),
                        lambda *idx: (layer, row_start(*idx), 0))


def _gate_rows(w):
    return jnp.where(_row_iota(w.shape) < N_GATE_COLS, w, 0.0).astype(BF16)


def _norm_proj(x, pw_ref, wg_ref, h_ref, g_ref):
    hb = (_rms(x) * pw_ref[...]).astype(BF16)
    h_ref[...] = hb
    g_ref[...] = _dot_nt(hb, _gate_rows(wg_ref[...]))


def _prenorm_body(x_ref, pw_ref, wg_ref, h_ref, g_ref):
    _norm_proj(x_ref[...], pw_ref, wg_ref, h_ref, g_ref)


def _post_body(x_ref, o_ref, qw_ref, xo_ref):
    xo_ref[...] = x_ref[...] + _rms(o_ref[...]) * qw_ref[...]


def _post_next_body(x_ref, o_ref, qw_ref, pw_ref, wg_ref, xo_ref, h_ref, g_ref):
    xn = x_ref[...] + _rms(o_ref[...]) * qw_ref[...]
    xo_ref[...] = xn
    _norm_proj(xn, pw_ref, wg_ref, h_ref, g_ref)


def _norm_out(seq):
    shapes = (jax.ShapeDtypeStruct((seq, D_MODEL), BF16), jax.ShapeDtypeStruct((seq, LANES), F32))
    specs = (_row_spec(TM_NORM, D_MODEL), _row_spec(TM_NORM, LANES))
    return shapes, specs


def _gate_spec(layer):
    return _wt_spec(layer, LANES, lambda *_: GATE_COL0)


def _prenorm(x, pre_w, layer, w_in_t):
    seq = x.shape[0]
    shapes, specs = _norm_out(seq)
    return pl.pallas_call(
        _prenorm_body,
        grid=(seq // TM_NORM,),
        in_specs=[_row_spec(TM_NORM, D_MODEL), _layer_spec(layer, (1, D_MODEL)), _gate_spec(layer)],
        out_specs=specs, out_shape=shapes,
        compiler_params=_params(("parallel",), VMEM_SMALL),
        name="prenorm",
    )(x, pre_w, w_in_t)


def _post(x, o, post_w, layer):
    seq = x.shape[0]
    return pl.pallas_call(
        _post_body,
        grid=(seq // TM_NORM,),
        in_specs=[_row_spec(TM_NORM, D_MODEL), _row_spec(TM_NORM, D_MODEL), _layer_spec(layer, (1, D_MODEL))],
        out_specs=_row_spec(TM_NORM, D_MODEL),
        out_shape=jax.ShapeDtypeStruct((seq, D_MODEL), F32),
        compiler_params=_params(("parallel",), VMEM_SMALL),
        name="post",
    )(x, o, post_w)


def _post_next(x, o, post_w, pre_w, layer, w_in_t):
    seq = x.shape[0]
    shapes, specs = _norm_out(seq)
    return pl.pallas_call(
        _post_next_body,
        grid=(seq // TM_NORM,),
        in_specs=[_row_spec(TM_NORM, D_MODEL), _row_spec(TM_NORM, D_MODEL), _layer_spec(layer, (1, D_MODEL)),
                  _layer_spec(layer + 1, (1, D_MODEL)), _gate_spec(layer + 1)],
        out_specs=(_row_spec(TM_NORM, D_MODEL),) + specs,
        out_shape=(jax.ShapeDtypeStruct((seq, D_MODEL), F32),) + shapes,
        compiler_params=_params(("parallel",), VMEM_SMALL),
        name="post_next",
    )(x, o, post_w, pre_w, w_in_t)


def _proj_rows(i, j):
    return _aligned_rows(j * (TN_PROJ // SUBLANES)
                         + jnp.where(j >= GATE_COL0 // TN_PROJ, N_GATE_COLS // SUBLANES, 0))


def _proj_body(h_ref, w_ref, o_ref):
    o_ref[...] = _dot_nt(h_ref[...], w_ref[...].astype(BF16))


def _proj(h, w_in_t, layer):
    m, k = h.shape
    return pl.pallas_call(
        _proj_body,
        grid=(m // TM_PROJ, MIX_W // TN_PROJ),
        in_specs=[pl.BlockSpec((TM_PROJ, k), lambda i, j: (i, 0), pipeline_mode=pl.Buffered(1)),
                  _wt_spec(layer, TN_PROJ, _proj_rows)],
        out_specs=pl.BlockSpec((TM_PROJ, TN_PROJ), lambda i, j: (i, j)),
        out_shape=jax.ShapeDtypeStruct((m, MIX_W), F32),
        compiler_params=_params(("parallel", "parallel"), VMEM_MM),
        name="proj",
    )(h, w_in_t)


def _out_body(a_ref, w_ref, o_ref):
    o_ref[...] = jnp.dot(a_ref[...], w_ref[...].astype(BF16), preferred_element_type=F32)


def _out_proj(merged, w_out, layer):
    m, k = merged.shape
    n = w_out.shape[2]
    return pl.pallas_call(
        _out_body,
        grid=(m // TM_OUT, n // TN_OUT),
        in_specs=[pl.BlockSpec((TM_OUT, k), lambda i, j: (i, 0)),
                  pl.BlockSpec((None, k, TN_OUT), lambda i, j: (layer, 0, j))],
        out_specs=pl.BlockSpec((TM_OUT, TN_OUT), lambda i, j: (i, j)),
        out_shape=jax.ShapeDtypeStruct((m, n), F32),
        compiler_params=_params(("parallel", "parallel"), VMEM_MM),
        name="out_proj",
    )(merged, w_out)


def _mlstm_chunk(rows, m_prev, q_ref, k_ref, v_ref, o_ref, z_ref, g_ref, gb_ref, nw_ref, y_ref, state_ref):
    i_pre = g_ref[rows, :] + gb_ref[...]
    f_pre = pltpu.roll(i_pre, LANES - HEADS, 1)
    lf = -_softplus(-f_pre)
    b = lf
    d = 1
    while d < CHUNK:
        b = b + _shift_rows(b, d, 0.0)
        d *= 2
    beta = i_pre - b
    cm = beta
    d = 1
    while d < CHUNK:
        cm = jnp.maximum(cm, _shift_rows(cm, d, -jnp.inf))
        d *= 2
    inter = b + m_prev
    m_t = jnp.maximum(inter, b + cm)
    alpha = b - m_t
    w_inter = jnp.exp(inter - m_t)
    e_negm = jnp.exp(-m_t)
    b_last = b[CHUNK - 1:CHUNK]
    m_new = jnp.maximum(b_last + m_prev, b_last + cm[CHUNK - 1:CHUNK])
    decay = jnp.exp(b_last + m_prev - m_new)
    ws = jnp.exp(b_last + beta - m_new)
    beta_t = beta.T

    shape = (CHUNK, CHUNK)
    tril = _row_iota(shape) >= _lane_iota(shape)
    lane0 = _lane_iota(shape) == 0
    yield

    sls = [slice(h * HEAD_DIM, (h + 1) * HEAD_DIM) for h in range(HEADS)]
    for first in range(0, HEADS, HEAD_GROUP):
        heads = range(first, first + HEAD_GROUP)
        qs = {h: q_ref[rows, sls[h]].astype(BF16) for h in heads}
        k_ts = {h: (k_ref[rows, sls[h]] * (HEAD_DIM ** -0.5)).T.astype(BF16) for h in heads}
        qks = {h: jnp.dot(qs[h], k_ts[h], preferred_element_type=F32) for h in heads}
        q_states = {h: jnp.dot(qs[h], state_ref[h].astype(BF16), preferred_element_type=F32)
                    for h in heads}
        scores = {h: qks[h] * jnp.exp(jnp.where(tril, alpha[:, h:h + 1] + beta_t[h:h + 1, :], -jnp.inf))
                  for h in heads}
        pvs = {h: jnp.dot(scores[h].astype(BF16), v_ref[rows, sls[h]].astype(BF16), preferred_element_type=F32)
               for h in heads}
        yield
        wi_cols = {h: w_inter[:, h:h + 1] for h in heads}
        dens = {h: jnp.sum(scores[h], axis=-1, keepdims=True)
                + wi_cols[h] * q_states[h][:, HEAD_DIM:HEAD_DIM + 1] for h in heads}
        r_dens = {h: 1.0 / jnp.maximum(jnp.abs(dens[h]), e_negm[:, h:h + 1]) for h in heads}
        hhs = {h: (pvs[h] + wi_cols[h] * q_states[h][:, :HEAD_DIM]) * r_dens[h] for h in heads}
        r_norms = {h: lax.rsqrt(jnp.mean(hhs[h] * hhs[h], axis=-1, keepdims=True) + EPS) for h in heads}
        for h in heads:
            sl = sls[h]
            y = hhs[h] * r_norms[h] * nw_ref[:, sl]
            y = (y * _sigmoid(o_ref[rows, sl])) * _silu(z_ref[rows, sl])
            y_ref[rows, sl] = y.astype(y_ref.dtype)
        yield
        for h in heads:
            ws_col = ws[:, h:h + 1]
            upd = jnp.concatenate([(ws_col * v_ref[rows, sls[h]]).astype(BF16),
                                   jnp.where(lane0, ws_col, 0.0).astype(BF16)], axis=1)
            state_ref[h] = decay[:, h:h + 1] * state_ref[h] + jnp.dot(k_ts[h], upd, preferred_element_type=F32)
        yield
    return m_new


def _mlstm_mix(q_ref, k_ref, v_ref, o_ref, z_ref, g_ref, gb_ref, nw_ref, y_ref, state_ref, m_ref):
    m = m_ref[...]
    for c in range(q_ref.shape[0] // CHUNK):
        rows = slice(c * CHUNK, (c + 1) * CHUNK)
        m = yield from _mlstm_chunk(rows, m, q_ref, k_ref, v_ref, o_ref, z_ref, g_ref, gb_ref, nw_ref, y_ref,
                                    state_ref)
    m_ref[...] = m


def _gmlp_mix(u_ref, v_ref, z_ref, lw_ref, lb_ref, ws_ref, bst_ref, y_ref):
    v = v_ref[...]
    mean = jnp.mean(v, axis=-1, keepdims=True)
    vc = v - mean
    rstd = lax.rsqrt(jnp.mean(vc * vc, axis=-1, keepdims=True) + EPS)
    shape = (CHUNK, CHUNK)
    tril = _row_iota(shape) >= _lane_iota(shape)
    yield
    for g in range(GROUPS):
        cols = slice(g * GROUP_W, (g + 1) * GROUP_W)
        w = jnp.where(tril, ws_ref[g], 0.0).astype(BF16)
        bias = bst_ref[:, g:g + 1]
        vn = ((v_ref[:, cols] - mean) * rstd * lw_ref[:, cols] + lb_ref[:, cols]).astype(BF16)
        for c in range(v.shape[0] // CHUNK):
            rows = slice(c * CHUNK, (c + 1) * CHUNK)
            sp = jnp.dot(w, vn[rows], preferred_element_type=F32) + bias
            y = u_ref[rows, cols] * sp * _silu(z_ref[rows, cols])
            y_ref[rows, cols] = y.astype(y_ref.dtype)
        yield


def _linear_scan(a, b, h0):
    rows, cols = a.shape
    groups = rows // SUBLANES
    a = a.reshape(groups, SUBLANES, cols)
    b = b.reshape(groups, SUBLANES, cols)
    sub = lax.broadcasted_iota(jnp.int32, a.shape, 1)
    d = 1
    while d < SUBLANES:
        keep = sub >= d
        b = b + a * jnp.where(keep, pltpu.roll(b, d, 1), 0.0)
        a = a * jnp.where(keep, pltpu.roll(a, d, 1), 1.0)
        d *= 2
    out = []
    for j in range(groups):
        hj = b[j] + a[j] * h0
        h0 = hj[SUBLANES - 1:]
        out.append(hj)
    return jnp.concatenate(out, axis=0)


def _rglru_mix(x_ref, z_ref, cw_ref, cb_ref, wg_ref, bg_ref, ap_ref, y_ref, prev_ref, carry_ref):
    step = pl.program_id(0)
    tm = x_ref.shape[0]
    sp = _softplus(-ap_ref[...])
    seq_start = (_row_iota((tm, LRU_BW)) == 0) & (step == 0)
    yield
    for n in range(LRU_BLOCKS):
        cols = slice(n * LRU_BW, (n + 1) * LRU_BW)
        x = x_ref[:, cols]
        xn = _causal_conv(x, prev_ref[:, cols], cw_ref[:, cols]) + cb_ref[:, cols]
        prev_ref[:, cols] = x[tm - SUBLANES:]
        gt = jnp.dot(xn.astype(BF16), wg_ref[n].astype(BF16), preferred_element_type=F32)
        r = _sigmoid(gt[:, :LRU_BW] + bg_ref[:, cols])
        ig = _sigmoid(gt[:, LRU_BW:] + bg_ref[:, BRANCH_W + n * LRU_BW:BRANCH_W + (n + 1) * LRU_BW])
        log_a = -LRU_C * r * sp[:, cols]
        a = jnp.exp(log_a)
        m2 = jnp.tanh(-log_a) * (a * a + 1.0)
        mult = jnp.where(m2 > 0.0, m2 * lax.rsqrt(m2), 0.0)
        mult = jnp.where(seq_start, 1.0, mult)
        hs = _linear_scan(a, mult * ig * xn, carry_ref[:, cols])
        carry_ref[:, cols] = hs[tm - 1:]
        y_ref[:, cols] = (hs * _silu(z_ref[:, cols])).astype(y_ref.dtype)
        yield


def _sconv_mix(b_ref, c_ref, x_ref, z_ref, w_ref, y_ref, prev_ref):
    tm = x_ref.shape[0]
    for n in range(BRANCH_W // LANES):
        cols = slice(n * LANES, (n + 1) * LANES)
        p = c_ref[:, cols] * x_ref[:, cols]
        conv = _causal_conv(p, prev_ref[:, cols], w_ref[:, cols])
        prev_ref[:, cols] = p[tm - SUBLANES:]
        y_ref[:, cols] = (b_ref[:, cols] * conv * _silu(z_ref[:, cols])).astype(y_ref.dtype)
        yield


def _alternate(*stage_generators):
    pending = list(stage_generators)
    while pending:
        for gen in list(pending):
            try:
                next(gen)
            except StopIteration:
                pending.remove(gen)


def _mixers_body(*refs):
    p = refs[:14]
    (g_ref, gb_ref, nw_ref, lw_ref, lb_ref, ws_ref, bst_ref, cw_ref, cb_ref, wg_ref, bg_ref, ap_ref, sw_ref,
     y_ref, state_ref, m_ref, lru_prev_ref, lru_carry_ref, conv_prev_ref) = refs[14:]

    @pl.when(pl.program_id(0) == 0)
    def _():
        for ref in (state_ref, m_ref, lru_prev_ref, lru_carry_ref, conv_prev_ref):
            ref[...] = jnp.zeros_like(ref)

    _alternate(_mlstm_mix(*p[0:5], g_ref, gb_ref, nw_ref, y_ref.at[0], state_ref, m_ref),
               _rglru_mix(*p[8:10], cw_ref, cb_ref, wg_ref, bg_ref, ap_ref, y_ref.at[2], lru_prev_ref,
                          lru_carry_ref),
               _gmlp_mix(*p[5:8], lw_ref, lb_ref, ws_ref, bst_ref, y_ref.at[1]),
               _sconv_mix(*p[10:14], sw_ref, y_ref.at[3], conv_prev_ref))


def _mixers(p_mix, gates, gate_bias, norm_w, ln_w, ln_b, w_s, b_s_t, conv_w, conv_b, w_gate, b_gate, a_param,
            sconv_w, layer):
    seq = p_mix.shape[0]
    tm = TM_MIX
    mix_specs = [pl.BlockSpec((tm, BRANCH_W), lambda i, c=c: (i, c)) for c in range(MIX_W // BRANCH_W)]
    vec = _layer_spec(layer, (1, BRANCH_W))
    return pl.pallas_call(
        _mixers_body,
        grid=(seq // tm,),
        in_specs=mix_specs + [
            _row_spec(tm, LANES), _full_spec((1, LANES)), vec,
            vec, vec, _layer_spec(layer, (GROUPS, CHUNK, CHUNK)), _full_spec((CHUNK, GROUPS)),
            _layer_spec(layer, conv_w.shape[1:]), vec, _layer_spec(layer, w_gate.shape[1:]),
            _layer_spec(layer, (1, 2 * BRANCH_W)), vec,
            _layer_spec(layer, sconv_w.shape[1:])],
        out_specs=pl.BlockSpec((N_BRANCH, tm, BRANCH_W), lambda i: (0, i, 0)),
        out_shape=jax.ShapeDtypeStruct((N_BRANCH, seq, BRANCH_W), BF16),
        scratch_shapes=[pltpu.VMEM((HEADS, HEAD_DIM, 2 * HEAD_DIM), F32), pltpu.VMEM((1, LANES), F32),
                        pltpu.VMEM((SUBLANES, BRANCH_W), F32), pltpu.VMEM((1, BRANCH_W), F32),
                        pltpu.VMEM((SUBLANES, BRANCH_W), F32)],
        compiler_params=_params(("arbitrary",), VMEM_MM),
        name="mixers",
    )(*([p_mix] * (MIX_W // BRANCH_W)), gates, gate_bias, norm_w, ln_w, ln_b, w_s, b_s_t, conv_w, conv_b,
      w_gate, b_gate, a_param, sconv_w)


def _merge_body(h_ref, y_ref, wg_ref, wb_ref, o_ref, pre_ref, acc_ref):
    s = pl.program_id(0)

    @pl.when(s == 0)
    def _():
        pre_ref[...] = jnp.zeros_like(pre_ref)
        acc_ref[...] = jnp.zeros_like(acc_ref)

    g_prev = lax.rem(jnp.maximum(s - 1, 0), N_BRANCH)
    gate = _sigmoid(pre_ref[lax.rem(s + 1, 2)])
    term = gate * jnp.dot(y_ref[...], wb_ref[...].astype(BF16), preferred_element_type=F32)
    acc = term + jnp.where(g_prev == 0, 0.0, acc_ref[...])
    acc_ref[...] = acc

    pre_ref[lax.rem(s, 2)] = _dot_nt(h_ref[...], wg_ref[...].astype(BF16))

    @pl.when(g_prev == N_BRANCH - 1)
    def _():
        o_ref[...] = acc.astype(o_ref.dtype)


def _merge(h, ys, w_in_t, w_branch, layer):
    seq = h.shape[0]
    n_tiles = D_MODEL // TN_MERGE
    per_row_tile = n_tiles * N_BRANCH
    n_items = (seq // TM_MERGE) * per_row_tile

    def item(s):
        return s // per_row_tile, lax.rem(s // N_BRANCH, n_tiles), lax.rem(s, N_BRANCH)

    def cur(s):
        return item(jnp.minimum(s, n_items - 1))

    def prev(s):
        return item(jnp.maximum(s - 1, 0))

    def gate_rows(s):
        _, j, g = cur(s)
        return _aligned_rows((MIX_W + N_GATE_COLS) // SUBLANES + (g * n_tiles + j) * (TN_MERGE // SUBLANES))

    return pl.pallas_call(
        _merge_body,
        grid=(n_items + 1,),
        in_specs=[pl.BlockSpec((TM_MERGE, D_MODEL), lambda s: (cur(s)[0], 0)),
                  pl.BlockSpec((None, TM_MERGE, BRANCH_W), lambda s: (prev(s)[2], prev(s)[0], 0)),
                  _wt_spec(layer, TN_MERGE, gate_rows),
                  pl.BlockSpec((None, None, BRANCH_W, TN_MERGE), lambda s: (layer, prev(s)[2], 0, prev(s)[1]))],
        out_specs=pl.BlockSpec((TM_MERGE, TN_MERGE), lambda s: (prev(s)[0], prev(s)[1])),
        out_shape=jax.ShapeDtypeStruct((seq, D_MODEL), BF16),
        scratch_shapes=[pltpu.VMEM((2, TM_MERGE, TN_MERGE), F32), pltpu.VMEM((TM_MERGE, TN_MERGE), F32)],
        compiler_params=_params(("arbitrary",), VMEM_MM),
        name="merge",
    )(h, ys, w_in_t, w_branch)


def kernel(x, pre_w, post_w, w_in, mlstm_gate_bias, mlstm_norm_w, gmlp_ln_w, gmlp_ln_b, gmlp_w_s, gmlp_b_s,
           rglru_conv_w, rglru_conv_b, rglru_w_gate, rglru_b_gate, rglru_a_param, sconv_w, w_branch, w_out):
    batch = x.shape[0]
    depth = w_in.shape[0]
    row = lambda p: p[:, None, :]
    pre_w, post_w = row(pre_w), row(post_w)
    norm_w, ln_w, ln_b = row(mlstm_norm_w), row(gmlp_ln_w), row(gmlp_ln_b)
    conv_b, b_gate, a_param = row(rglru_conv_b), row(rglru_b_gate), row(rglru_a_param)
    gate_bias = jnp.pad(mlstm_gate_bias, ((0, 0), (0, LANES - N_GATE_COLS)))
    b_s_t = jnp.swapaxes(gmlp_b_s, 1, 2)
    w_in_t = jnp.swapaxes(w_in, 1, 2)
    outs = []
    for bi in range(batch):
        xb = x[bi]
        h, gates = _prenorm(xb, pre_w, 0, w_in_t)
        for l in range(depth):
            p_mix = _proj(h, w_in_t, l)
            ys = _mixers(p_mix, gates, gate_bias[l][None], norm_w, ln_w, ln_b, gmlp_w_s, b_s_t[l],
                         rglru_conv_w, conv_b, rglru_w_gate, b_gate, a_param, sconv_w, l)
            merged = _merge(h, ys, w_in_t, w_branch, l)
            out = _out_proj(merged, w_out, l)
            if l + 1 < depth:
                xb, h, gates = _post_next(xb, out, post_w, pre_w, l, w_in_t)
            else:
                xb = _post(xb, out, post_w, l)
        outs.append(xb)
    return jnp.stack(outs, axis=0)
```

```python
import jax
import jax.numpy as jnp
from jax import lax
from jax.experimental import pallas as pl
from jax.experimental.pallas import tpu as pltpu

F32 = jnp.float32
BF16 = jnp.bfloat16

D_MODEL = 4096
BRANCH_W = D_MODEL // 4
N_BRANCH = 4
HEADS = 8
HEAD_DIM = BRANCH_W // HEADS
CHUNK = 128
GROUPS = 8
GROUP_W = BRANCH_W // GROUPS
LRU_BLOCKS = 8
LRU_BW = BRANCH_W // LRU_BLOCKS
LRU_C = 8.0
EPS = 1e-6

LANES = 128
SUBLANES = 8
N_GATE_COLS = 2 * HEADS
GATE_COL0 = 5 * BRANCH_W
MIX_W = 14 * BRANCH_W

TM_NORM = 256
TM_PROJ = 2048
TN_PROJ = 512
TM_MERGE = 1024
TN_MERGE = 512
TM_OUT = 1024
TN_OUT = 512
TM_MIX = 2 * CHUNK
VMEM_MM = 52 * 1024 * 1024
VMEM_SMALL = 40 * 1024 * 1024


def _params(semantics, vmem):
    return pltpu.CompilerParams(dimension_semantics=semantics, vmem_limit_bytes=vmem)


def _sigmoid(x):
    return 1.0 / (1.0 + jnp.exp(-x))


def _silu(x):
    return x * _sigmoid(x)


def _softplus(x):
    return jnp.maximum(x, 0.0) + jnp.log1p(jnp.exp(-jnp.abs(x)))


def _row_iota(shape):
    return lax.broadcasted_iota(jnp.int32, shape, 0)


def _lane_iota(shape):
    return lax.broadcasted_iota(jnp.int32, shape, len(shape) - 1)


def _shift_rows(x, d, fill):
    rows, cols = x.shape
    if d % SUBLANES == 0:
        return jnp.concatenate([jnp.full((d, cols), fill, x.dtype), x[:rows - d]], axis=0)
    return jnp.where(_row_iota(x.shape) >= d, pltpu.roll(x, d, 0), fill)


def _causal_conv(x, prev, w):
    taps = w.shape[0]

    def conv(arr):
        y = pltpu.roll(arr, taps - 1, 0) * w[0:1]
        for k in range(1, taps - 1):
            y = y + pltpu.roll(arr, taps - 1 - k, 0) * w[k:k + 1]
        return y + arr * w[taps - 1:taps]

    body = conv(x)
    head = conv(jnp.concatenate([prev, x[:SUBLANES]], axis=0))[SUBLANES:]
    return jnp.concatenate([head, body[SUBLANES:]], axis=0)


def _rms(x):
    return x * lax.rsqrt(jnp.mean(x * x, axis=-1, keepdims=True) + EPS)


def _row_spec(tm, width):
    return pl.BlockSpec((tm, width), lambda i: (i, 0))


def _full_spec(shape):
    return pl.BlockSpec(shape, lambda *_: (0,) * len(shape))


def _layer_spec(layer, shape):
    return pl.BlockSpec((None,) + shape, lambda *_: (layer,) + (0,) * len(shape))


def _dot_nt(a, b_t):
    return lax.dot_general(a, b_t, (((1,), (1,)), ((), ())), preferred_element_type=F32)


def _aligned_rows(octets):
    return octets * SUBLANES


def _wt_spec(layer, rows, row_start):
    return pl.BlockSpec((None, pl.Element(rows), pl.Element(D_MODEL)),
                        lambda *idx: (layer, row_start(*idx), 0))


def _gate_rows(w):
    return jnp.where(_row_iota(w.shape) < N_GATE_COLS, w, 0.0).astype(BF16)


def _norm_proj(x, pw_ref, wg_ref, h_ref, g_ref):
    hb = (_rms(x) * pw_ref[...]).astype(BF16)
    h_ref[...] = hb
    g_ref[...] = _dot_nt(hb, _gate_rows(wg_ref[...]))


def _prenorm_body(x_ref, pw_ref, wg_ref, h_ref, g_ref):
    _norm_proj(x_ref[...], pw_ref, wg_ref, h_ref, g_ref)


def _post_body(x_ref, o_ref, qw_ref, xo_ref):
    xo_ref[...] = x_ref[...] + _rms(o_ref[...]) * qw_ref[...]


def _post_next_body(x_ref, o_ref, qw_ref, pw_ref, wg_ref, xo_ref, h_ref, g_ref):
    xn = x_ref[...] + _rms(o_ref[...]) * qw_ref[...]
    xo_ref[...] = xn
    _norm_proj(xn, pw_ref, wg_ref, h_ref, g_ref)


def _norm_out(seq):
    shapes = (jax.ShapeDtypeStruct((seq, D_MODEL), BF16), jax.ShapeDtypeStruct((seq, LANES), F32))
    specs = (_row_spec(TM_NORM, D_MODEL), _row_spec(TM_NORM, LANES))
    return shapes, specs


def _gate_spec(layer):
    return _wt_spec(layer, LANES, lambda *_: GATE_COL0)


def _prenorm(x, pre_w, layer, w_in_t):
    seq = x.shape[0]
    shapes, specs = _norm_out(seq)
    return pl.pallas_call(
        _prenorm_body,
        grid=(seq // TM_NORM,),
        in_specs=[_row_spec(TM_NORM, D_MODEL), _layer_spec(layer, (1, D_MODEL)), _gate_spec(layer)],
        out_specs=specs, out_shape=shapes,
        compiler_params=_params(("parallel",), VMEM_SMALL),
        name="prenorm",
    )(x, pre_w, w_in_t)


def _post(x, o, post_w, layer):
    seq = x.shape[0]
    return pl.pallas_call(
        _post_body,
        grid=(seq // TM_NORM,),
        in_specs=[_row_spec(TM_NORM, D_MODEL), _row_spec(TM_NORM, D_MODEL), _layer_spec(layer, (1, D_MODEL))],
        out_specs=_row_spec(TM_NORM, D_MODEL),
        out_shape=jax.ShapeDtypeStruct((seq, D_MODEL), F32),
        compiler_params=_params(("parallel",), VMEM_SMALL),
        name="post",
    )(x, o, post_w)


def _post_next(x, o, post_w, pre_w, layer, w_in_t):
    seq = x.shape[0]
    shapes, specs = _norm_out(seq)
    return pl.pallas_call(
        _post_next_body,
        grid=(seq // TM_NORM,),
        in_specs=[_row_spec(TM_NORM, D_MODEL), _row_spec(TM_NORM, D_MODEL), _layer_spec(layer, (1, D_MODEL)),
                  _layer_spec(layer + 1, (1, D_MODEL)), _gate_spec(layer + 1)],
        out_specs=(_row_spec(TM_NORM, D_MODEL),) + specs,
        out_shape=(jax.ShapeDtypeStruct((seq, D_MODEL), F32),) + shapes,
        compiler_params=_params(("parallel",), VMEM_SMALL),
        name="post_next",
    )(x, o, post_w, pre_w, w_in_t)


def _proj_rows(i, j):
    return _aligned_rows(j * (TN_PROJ // SUBLANES)
                         + jnp.where(j >= GATE_COL0 // TN_PROJ, N_GATE_COLS // SUBLANES, 0))


def _proj_body(h_ref, w_ref, o_ref):
    o_ref[...] = _dot_nt(h_ref[...], w_ref[...].astype(BF16))


def _proj(h, w_in_t, layer):
    m, k = h.shape
    return pl.pallas_call(
        _proj_body,
        grid=(m // TM_PROJ, MIX_W // TN_PROJ),
        in_specs=[pl.BlockSpec((TM_PROJ, k), lambda i, j: (i, 0), pipeline_mode=pl.Buffered(1)),
                  _wt_spec(layer, TN_PROJ, _proj_rows)],
        out_specs=pl.BlockSpec((TM_PROJ, TN_PROJ), lambda i, j: (i, j)),
        out_shape=jax.ShapeDtypeStruct((m, MIX_W), F32),
        compiler_params=_params(("parallel", "parallel"), VMEM_MM),
        name="proj",
    )(h, w_in_t)


def _out_body(a_ref, w_ref, o_ref):
    o_ref[...] = jnp.dot(a_ref[...], w_ref[...].astype(BF16), preferred_element_type=F32)


def _out_proj(merged, w_out, layer):
    m, k = merged.shape
    n = w_out.shape[2]
    return pl.pallas_call(
        _out_body,
        grid=(m // TM_OUT, n // TN_OUT),
        in_specs=[pl.BlockSpec((TM_OUT, k), lambda i, j: (i, 0)),
                  pl.BlockSpec((None, k, TN_OUT), lambda i, j: (layer, 0, j))],
        out_specs=pl.BlockSpec((TM_OUT, TN_OUT), lambda i, j: (i, j)),
        out_shape=jax.ShapeDtypeStruct((m, n), F32),
        compiler_params=_params(("parallel", "parallel"), VMEM_MM),
        name="out_proj",
    )(merged, w_out)


def _mlstm_chunk(rows, m_prev, q_ref, k_ref, v_ref, o_ref, z_ref, g_ref, gb_ref, nw_ref, y_ref, state_ref):
    i_pre = g_ref[rows, :] + gb_ref[...]
    f_pre = pltpu.roll(i_pre, LANES - HEADS, 1)
    lf = -_softplus(-f_pre)
    b = lf
    d = 1
    while d < CHUNK:
        b = b + _shift_rows(b, d, 0.0)
        d *= 2
    beta = i_pre - b
    cm = beta
    d = 1
    while d < CHUNK:
        cm = jnp.maximum(cm, _shift_rows(cm, d, -jnp.inf))
        d *= 2
    inter = b + m_prev
    m_t = jnp.maximum(inter, b + cm)
    alpha = b - m_t
    w_inter = jnp.exp(inter - m_t)
    e_negm = jnp.exp(-m_t)
    b_last = b[CHUNK - 1:CHUNK]
    m_new = jnp.maximum(b_last + m_prev, b_last + cm[CHUNK - 1:CHUNK])
    decay = jnp.exp(b_last + m_prev - m_new)
    ws = jnp.exp(b_last + beta - m_new)
    beta_t = beta.T

    shape = (CHUNK, CHUNK)
    tril = _row_iota(shape) >= _lane_iota(shape)
    lane0 = _lane_iota(shape) == 0
    yield

    heads = range(HEADS)
    sls = [slice(h * HEAD_DIM, (h + 1) * HEAD_DIM) for h in heads]
    qs = [q_ref[rows, sl].astype(BF16) for sl in sls]
    k_ts = [(k_ref[rows, sl] * (HEAD_DIM ** -0.5)).T.astype(BF16) for sl in sls]
    qks = [jnp.dot(qs[h], k_ts[h], preferred_element_type=F32) for h in heads]
    q_states = [jnp.dot(qs[h], state_ref[h].astype(BF16), preferred_element_type=F32) for h in heads]
    scores = [qks[h] * jnp.exp(jnp.where(tril, alpha[:, h:h + 1] + beta_t[h:h + 1, :], -jnp.inf))
              for h in heads]
    pvs = [jnp.dot(scores[h].astype(BF16), v_ref[rows, sls[h]].astype(BF16), preferred_element_type=F32)
           for h in heads]
    yield
    wi_cols = [w_inter[:, h:h + 1] for h in heads]
    dens = [jnp.sum(scores[h], axis=-1, keepdims=True) + wi_cols[h] * q_states[h][:, HEAD_DIM:HEAD_DIM + 1]
            for h in heads]
    r_dens = [1.0 / jnp.maximum(jnp.abs(dens[h]), e_negm[:, h:h + 1]) for h in heads]
    hhs = [(pvs[h] + wi_cols[h] * q_states[h][:, :HEAD_DIM]) * r_dens[h] for h in heads]
    r_norms = [lax.rsqrt(jnp.mean(hh * hh, axis=-1, keepdims=True) + EPS) for hh in hhs]
    for h in heads:
        sl = sls[h]
        y = hhs[h] * r_norms[h] * nw_ref[:, sl]
        y = (y * _sigmoid(o_ref[rows, sl])) * _silu(z_ref[rows, sl])
        y_ref[rows, sl] = y.astype(y_ref.dtype)
    yield
    for h in heads:
        ws_col = ws[:, h:h + 1]
        upd = jnp.concatenate([(ws_col * v_ref[rows, sls[h]]).astype(BF16),
                               jnp.where(lane0, ws_col, 0.0).astype(BF16)], axis=1)
        state_ref[h] = decay[:, h:h + 1] * state_ref[h] + jnp.dot(k_ts[h], upd, preferred_element_type=F32)
    yield
    return m_new


def _mlstm_mix(q_ref, k_ref, v_ref, o_ref, z_ref, g_ref, gb_ref, nw_ref, y_ref, state_ref, m_ref):
    m = m_ref[...]
    for c in range(q_ref.shape[0] // CHUNK):
        rows = slice(c * CHUNK, (c + 1) * CHUNK)
        m = yield from _mlstm_chunk(rows, m, q_ref, k_ref, v_ref, o_ref, z_ref, g_ref, gb_ref, nw_ref, y_ref,
                                    state_ref)
    m_ref[...] = m


def _gmlp_mix(u_ref, v_ref, z_ref, lw_ref, lb_ref, ws_ref, bst_ref, y_ref):
    v = v_ref[...]
    mean = jnp.mean(v, axis=-1, keepdims=True)
    vc = v - mean
    rstd = lax.rsqrt(jnp.mean(vc * vc, axis=-1, keepdims=True) + EPS)
    shape = (CHUNK, CHUNK)
    tril = _row_iota(shape) >= _lane_iota(shape)
    yield
    for g in range(GROUPS):
        cols = slice(g * GROUP_W, (g + 1) * GROUP_W)
        w = jnp.where(tril, ws_ref[g], 0.0).astype(BF16)
        bias = bst_ref[:, g:g + 1]
        vn = ((v_ref[:, cols] - mean) * rstd * lw_ref[:, cols] + lb_ref[:, cols]).astype(BF16)
        for c in range(v.shape[0] // CHUNK):
            rows = slice(c * CHUNK, (c + 1) * CHUNK)
            sp = jnp.dot(w, vn[rows], preferred_element_type=F32) + bias
            y = u_ref[rows, cols] * sp * _silu(z_ref[rows, cols])
            y_ref[rows, cols] = y.astype(y_ref.dtype)
        yield


def _linear_scan(a, b, h0):
    rows, cols = a.shape
    groups = rows // SUBLANES
    a = a.reshape(groups, SUBLANES, cols)
    b = b.reshape(groups, SUBLANES, cols)
    sub = lax.broadcasted_iota(jnp.int32, a.shape, 1)
    d = 1
    while d < SUBLANES:
        keep = sub >= d
        b = b + a * jnp.where(keep, pltpu.roll(b, d, 1), 0.0)
        a = a * jnp.where(keep, pltpu.roll(a, d, 1), 1.0)
        d *= 2
    out = []
    for j in range(groups):
        hj = b[j] + a[j] * h0
        h0 = hj[SUBLANES - 1:]
        out.append(hj)
    return jnp.concatenate(out, axis=0)


def _rglru_mix(x_ref, z_ref, cw_ref, cb_ref, wg_ref, bg_ref, ap_ref, y_ref, prev_ref, carry_ref):
    step = pl.program_id(0)
    tm = x_ref.shape[0]
    sp = _softplus(-ap_ref[...])
    seq_start = (_row_iota((tm, LRU_BW)) == 0) & (step == 0)
    yield
    for n in range(LRU_BLOCKS):
        cols = slice(n * LRU_BW, (n + 1) * LRU_BW)
        x = x_ref[:, cols]
        xn = _causal_conv(x, prev_ref[:, cols], cw_ref[:, cols]) + cb_ref[:, cols]
        prev_ref[:, cols] = x[tm - SUBLANES:]
        gt = jnp.dot(xn.astype(BF16), wg_ref[n].astype(BF16), preferred_element_type=F32)
        r = _sigmoid(gt[:, :LRU_BW] + bg_ref[:, cols])
        ig = _sigmoid(gt[:, LRU_BW:] + bg_ref[:, BRANCH_W + n * LRU_BW:BRANCH_W + (n + 1) * LRU_BW])
        log_a = -LRU_C * r * sp[:, cols]
        a = jnp.exp(log_a)
        m2 = jnp.tanh(-log_a) * (a * a + 1.0)
        mult = jnp.where(m2 > 0.0, m2 * lax.rsqrt(m2), 0.0)
        mult = jnp.where(seq_start, 1.0, mult)
        hs = _linear_scan(a, mult * ig * xn, carry_ref[:, cols])
        carry_ref[:, cols] = hs[tm - 1:]
        y_ref[:, cols] = (hs * _silu(z_ref[:, cols])).astype(y_ref.dtype)
        yield


def _sconv_mix(b_ref, c_ref, x_ref, z_ref, w_ref, y_ref, prev_ref):
    tm = x_ref.shape[0]
    for n in range(BRANCH_W // LANES):
        cols = slice(n * LANES, (n + 1) * LANES)
        p = c_ref[:, cols] * x_ref[:, cols]
        conv = _causal_conv(p, prev_ref[:, cols], w_ref[:, cols])
        prev_ref[:, cols] = p[tm - SUBLANES:]
        y_ref[:, cols] = (b_ref[:, cols] * conv * _silu(z_ref[:, cols])).astype(y_ref.dtype)
        yield


def _alternate(*stage_generators):
    pending = list(stage_generators)
    while pending:
        for gen in list(pending):
            try:
                next(gen)
            except StopIteration:
                pending.remove(gen)


def _mixers_body(*refs):
    p = refs[:14]
    (g_ref, gb_ref, nw_ref, lw_ref, lb_ref, ws_ref, bst_ref, cw_ref, cb_ref, wg_ref, bg_ref, ap_ref, sw_ref,
     y_ref, state_ref, m_ref, lru_prev_ref, lru_carry_ref, conv_prev_ref) = refs[14:]

    @pl.when(pl.program_id(0) == 0)
    def _():
        for ref in (state_ref, m_ref, lru_prev_ref, lru_carry_ref, conv_prev_ref):
            ref[...] = jnp.zeros_like(ref)

    _alternate(_mlstm_mix(*p[0:5], g_ref, gb_ref, nw_ref, y_ref.at[0], state_ref, m_ref),
               _rglru_mix(*p[8:10], cw_ref, cb_ref, wg_ref, bg_ref, ap_ref, y_ref.at[2], lru_prev_ref,
                          lru_carry_ref),
               _gmlp_mix(*p[5:8], lw_ref, lb_ref, ws_ref, bst_ref, y_ref.at[1]),
               _sconv_mix(*p[10:14], sw_ref, y_ref.at[3], conv_prev_ref))


def _mixers(p_mix, gates, gate_bias, norm_w, ln_w, ln_b, w_s, b_s_t, conv_w, conv_b, w_gate, b_gate, a_param,
            sconv_w, layer):
    seq = p_mix.shape[0]
    tm = TM_MIX
    mix_specs = [pl.BlockSpec((tm, BRANCH_W), lambda i, c=c: (i, c)) for c in range(MIX_W // BRANCH_W)]
    vec = _layer_spec(layer, (1, BRANCH_W))
    return pl.pallas_call(
        _mixers_body,
        grid=(seq // tm,),
        in_specs=mix_specs + [
            _row_spec(tm, LANES), _full_spec((1, LANES)), vec,
            vec, vec, _layer_spec(layer, (GROUPS, CHUNK, CHUNK)), _full_spec((CHUNK, GROUPS)),
            _layer_spec(layer, conv_w.shape[1:]), vec, _layer_spec(layer, w_gate.shape[1:]),
            _layer_spec(layer, (1, 2 * BRANCH_W)), vec,
            _layer_spec(layer, sconv_w.shape[1:])],
        out_specs=pl.BlockSpec((N_BRANCH, tm, BRANCH_W), lambda i: (0, i, 0)),
        out_shape=jax.ShapeDtypeStruct((N_BRANCH, seq, BRANCH_W), BF16),
        scratch_shapes=[pltpu.VMEM((HEADS, HEAD_DIM, 2 * HEAD_DIM), F32), pltpu.VMEM((1, LANES), F32),
                        pltpu.VMEM((SUBLANES, BRANCH_W), F32), pltpu.VMEM((1, BRANCH_W), F32),
                        pltpu.VMEM((SUBLANES, BRANCH_W), F32)],
        compiler_params=_params(("arbitrary",), VMEM_MM),
        name="mixers",
    )(*([p_mix] * (MIX_W // BRANCH_W)), gates, gate_bias, norm_w, ln_w, ln_b, w_s, b_s_t, conv_w, conv_b,
      w_gate, b_gate, a_param, sconv_w)


def _merge_body(h_ref, y_ref, wg_ref, wb_ref, o_ref, pre_ref, acc_ref):
    s = pl.program_id(0)

    @pl.when(s == 0)
    def _():
        pre_ref[...] = jnp.zeros_like(pre_ref)
        acc_ref[...] = jnp.zeros_like(acc_ref)

    g_prev = lax.rem(jnp.maximum(s - 1, 0), N_BRANCH)
    gate = _sigmoid(pre_ref[lax.rem(s + 1, 2)])
    term = gate * jnp.dot(y_ref[...], wb_ref[...].astype(BF16), preferred_element_type=F32)
    acc = term + jnp.where(g_prev == 0, 0.0, acc_ref[...])
    acc_ref[...] = acc

    pre_ref[lax.rem(s, 2)] = _dot_nt(h_ref[...], wg_ref[...].astype(BF16))

    @pl.when(g_prev == N_BRANCH - 1)
    def _():
        o_ref[...] = acc.astype(o_ref.dtype)


def _merge(h, ys, w_in_t, w_branch, layer):
    seq = h.shape[0]
    n_tiles = D_MODEL // TN_MERGE
    per_row_tile = n_tiles * N_BRANCH
    n_items = (seq // TM_MERGE) * per_row_tile

    def item(s):
        return s // per_row_tile, lax.rem(s // N_BRANCH, n_tiles), lax.rem(s, N_BRANCH)

    def cur(s):
        return item(jnp.minimum(s, n_items - 1))

    def prev(s):
        return item(jnp.maximum(s - 1, 0))

    def gate_rows(s):
        _, j, g = cur(s)
        return _aligned_rows((MIX_W + N_GATE_COLS) // SUBLANES + (g * n_tiles + j) * (TN_MERGE // SUBLANES))

    return pl.pallas_call(
        _merge_body,
        grid=(n_items + 1,),
        in_specs=[pl.BlockSpec((TM_MERGE, D_MODEL), lambda s: (cur(s)[0], 0)),
                  pl.BlockSpec((None, TM_MERGE, BRANCH_W), lambda s: (prev(s)[2], prev(s)[0], 0)),
                  _wt_spec(layer, TN_MERGE, gate_rows),
                  pl.BlockSpec((None, None, BRANCH_W, TN_MERGE), lambda s: (layer, prev(s)[2], 0, prev(s)[1]))],
        out_specs=pl.BlockSpec((TM_MERGE, TN_MERGE), lambda s: (prev(s)[0], prev(s)[1])),
        out_shape=jax.ShapeDtypeStruct((seq, D_MODEL), BF16),
        scratch_shapes=[pltpu.VMEM((2, TM_MERGE, TN_MERGE), F32), pltpu.VMEM((TM_MERGE, TN_MERGE), F32)],
        compiler_params=_params(("arbitrary",), VMEM_MM),
        name="merge",
    )(h, ys, w_in_t, w_branch)


def kernel(x, pre_w, post_w, w_in, mlstm_gate_bias, mlstm_norm_w, gmlp_ln_w, gmlp_ln_b, gmlp_w_s, gmlp_b_s,
           rglru_conv_w, rglru_conv_b, rglru_w_gate, rglru_b_gate, rglru_a_param, sconv_w, w_branch, w_out):
    batch = x.shape[0]
    depth = w_in.shape[0]
    row = lambda p: p[:, None, :]
    pre_w, post_w = row(pre_w), row(post_w)
    norm_w, ln_w, ln_b = row(mlstm_norm_w), row(gmlp_ln_w), row(gmlp_ln_b)
    conv_b, b_gate, a_param = row(rglru_conv_b), row(rglru_b_gate), row(rglru_a_param)
    gate_bias = jnp.pad(mlstm_gate_bias, ((0, 0), (0, LANES - N_GATE_COLS)))
    b_s_t = jnp.swapaxes(gmlp_b_s, 1, 2)
    w_in_t = jnp.swapaxes(w_in, 1, 2)
    outs = []
    for bi in range(batch):
        xb = x[bi]
        h, gates = _prenorm(xb, pre_w, 0, w_in_t)
        for l in range(depth):
            p_mix = _proj(h, w_in_t, l)
            ys = _mixers(p_mix, gates, gate_bias[l][None], norm_w, ln_w, ln_b, gmlp_w_s, b_s_t[l],
                         rglru_conv_w, conv_b, rglru_w_gate, b_gate, a_param, sconv_w, l)
            merged = _merge(h, ys, w_in_t, w_branch, l)
            out = _out_proj(merged, w_out, l)
            if l + 1 < depth:
                xb, h, gates = _post_next(xb, out, post_w, pre_w, l, w_in_t)
            else:
                xb = _post(xb, out, post_w, l)
        outs.append(xb)
    return jnp.stack(outs, axis=0)
```

```python
import jax
import jax.numpy as jnp
from jax import lax
from jax.experimental import pallas as pl
from jax.experimental.pallas import tpu as pltpu

F32 = jnp.float32
BF16 = jnp.bfloat16

D_MODEL = 4096
BRANCH_W = D_MODEL // 4
N_BRANCH = 4
HEADS = 8
HEAD_DIM = BRANCH_W // HEADS
CHUNK = 128
GROUPS = 8
GROUP_W = BRANCH_W // GROUPS
LRU_BLOCKS = 8
LRU_BW = BRANCH_W // LRU_BLOCKS
LRU_C = 8.0
EPS = 1e-6

LANES = 128
SUBLANES = 8
N_GATE_COLS = 2 * HEADS
GATE_COL0 = 5 * BRANCH_W
MIX_W = 14 * BRANCH_W

TM_NORM = 256
TM_PROJ = 2048
TN_PROJ = 512
TM_MERGE = 1024
TN_MERGE = 512
TM_OUT = 1024
TN_OUT = 512
TM_MIX = 2 * CHUNK
VMEM_MM = 52 * 1024 * 1024
VMEM_SMALL = 40 * 1024 * 1024


def _params(semantics, vmem):
    return pltpu.CompilerParams(dimension_semantics=semantics, vmem_limit_bytes=vmem)


def _sigmoid(x):
    return 1.0 / (1.0 + jnp.exp(-x))


def _silu(x):
    return x * _sigmoid(x)


def _softplus(x):
    return jnp.maximum(x, 0.0) + jnp.log1p(jnp.exp(-jnp.abs(x)))


def _row_iota(shape):
    return lax.broadcasted_iota(jnp.int32, shape, 0)


def _lane_iota(shape):
    return lax.broadcasted_iota(jnp.int32, shape, len(shape) - 1)


def _shift_rows(x, d, fill):
    rows, cols = x.shape
    if d % SUBLANES == 0:
        return jnp.concatenate([jnp.full((d, cols), fill, x.dtype), x[:rows - d]], axis=0)
    return jnp.where(_row_iota(x.shape) >= d, pltpu.roll(x, d, 0), fill)


def _causal_conv(x, prev, w):
    taps = w.shape[0]

    def conv(arr):
        y = pltpu.roll(arr, taps - 1, 0) * w[0:1]
        for k in range(1, taps - 1):
            y = y + pltpu.roll(arr, taps - 1 - k, 0) * w[k:k + 1]
        return y + arr * w[taps - 1:taps]

    body = conv(x)
    head = conv(jnp.concatenate([prev, x[:SUBLANES]], axis=0))[SUBLANES:]
    return jnp.concatenate([head, body[SUBLANES:]], axis=0)


def _rms(x):
    return x * lax.rsqrt(jnp.mean(x * x, axis=-1, keepdims=True) + EPS)


def _row_spec(tm, width):
    return pl.BlockSpec((tm, width), lambda i: (i, 0))


def _full_spec(shape):
    return pl.BlockSpec(shape, lambda *_: (0,) * len(shape))


def _layer_spec(layer, shape):
    return pl.BlockSpec((None,) + shape, lambda *_: (layer,) + (0,) * len(shape))


def _dot_nt(a, b_t):
    return lax.dot_general(a, b_t, (((1,), (1,)), ((), ())), preferred_element_type=F32)


def _aligned_rows(octets):
    return octets * SUBLANES


def _wt_spec(layer, rows, row_start):
    return pl.BlockSpec((None, pl.Element(rows), pl.Element(D_MODEL)),
                        lambda *idx: (layer, row_start(*idx), 0))


def _gate_rows(w):
    return jnp.where(_row_iota(w.shape) < N_GATE_COLS, w, 0.0).astype(BF16)


def _norm_proj(x, pw_ref, wg_ref, h_ref, g_ref):
    hb = (_rms(x) * pw_ref[...]).astype(BF16)
    h_ref[...] = hb
    g_ref[...] = _dot_nt(hb, _gate_rows(wg_ref[...]))


def _prenorm_body(x_ref, pw_ref, wg_ref, h_ref, g_ref):
    _norm_proj(x_ref[...], pw_ref, wg_ref, h_ref, g_ref)


def _post_body(x_ref, o_ref, qw_ref, xo_ref):
    xo_ref[...] = x_ref[...] + _rms(o_ref[...]) * qw_ref[...]


def _post_next_body(x_ref, o_ref, qw_ref, pw_ref, wg_ref, xo_ref, h_ref, g_ref):
    xn = x_ref[...] + _rms(o_ref[...]) * qw_ref[...]
    xo_ref[...] = xn
    _norm_proj(xn, pw_ref, wg_ref, h_ref, g_ref)


def _norm_out(seq):
    shapes = (jax.ShapeDtypeStruct((seq, D_MODEL), BF16), jax.ShapeDtypeStruct((seq, LANES), F32))
    specs = (_row_spec(TM_NORM, D_MODEL), _row_spec(TM_NORM, LANES))
    return shapes, specs


def _gate_spec(layer):
    return _wt_spec(layer, LANES, lambda *_: GATE_COL0)


def _prenorm(x, pre_w, layer, w_in_t):
    seq = x.shape[0]
    shapes, specs = _norm_out(seq)
    return pl.pallas_call(
        _prenorm_body,
        grid=(seq // TM_NORM,),
        in_specs=[_row_spec(TM_NORM, D_MODEL), _layer_spec(layer, (1, D_MODEL)), _gate_spec(layer)],
        out_specs=specs, out_shape=shapes,
        compiler_params=_params(("parallel",), VMEM_SMALL),
        name="prenorm",
    )(x, pre_w, w_in_t)


def _post(x, o, post_w, layer):
    seq = x.shape[0]
    return pl.pallas_call(
        _post_body,
        grid=(seq // TM_NORM,),
        in_specs=[_row_spec(TM_NORM, D_MODEL), _row_spec(TM_NORM, D_MODEL), _layer_spec(layer, (1, D_MODEL))],
        out_specs=_row_spec(TM_NORM, D_MODEL),
        out_shape=jax.ShapeDtypeStruct((seq, D_MODEL), F32),
        compiler_params=_params(("parallel",), VMEM_SMALL),
        name="post",
    )(x, o, post_w)


def _post_next(x, o, post_w, pre_w, layer, w_in_t):
    seq = x.shape[0]
    shapes, specs = _norm_out(seq)
    return pl.pallas_call(
        _post_next_body,
        grid=(seq // TM_NORM,),
        in_specs=[_row_spec(TM_NORM, D_MODEL), _row_spec(TM_NORM, D_MODEL), _layer_spec(layer, (1, D_MODEL)),
                  _layer_spec(layer + 1, (1, D_MODEL)), _gate_spec(layer + 1)],
        out_specs=(_row_spec(TM_NORM, D_MODEL),) + specs,
        out_shape=(jax.ShapeDtypeStruct((seq, D_MODEL), F32),) + shapes,
        compiler_params=_params(("parallel",), VMEM_SMALL),
        name="post_next",
    )(x, o, post_w, pre_w, w_in_t)


def _proj_rows(i, j):
    return _aligned_rows(j * (TN_PROJ // SUBLANES)
                         + jnp.where(j >= GATE_COL0 // TN_PROJ, N_GATE_COLS // SUBLANES, 0))


def _proj_body(h_ref, w_ref, o_ref):
    o_ref[...] = _dot_nt(h_ref[...], w_ref[...].astype(BF16))


def _proj(h, w_in_t, layer):
    m, k = h.shape
    return pl.pallas_call(
        _proj_body,
        grid=(m // TM_PROJ, MIX_W // TN_PROJ),
        in_specs=[pl.BlockSpec((TM_PROJ, k), lambda i, j: (i, 0), pipeline_mode=pl.Buffered(1)),
                  _wt_spec(layer, TN_PROJ, _proj_rows)],
        out_specs=pl.BlockSpec((TM_PROJ, TN_PROJ), lambda i, j: (i, j)),
        out_shape=jax.ShapeDtypeStruct((m, MIX_W), F32),
        compiler_params=_params(("parallel", "parallel"), VMEM_MM),
        name="proj",
    )(h, w_in_t)


def _out_body(a_ref, w_ref, o_ref):
    o_ref[...] = jnp.dot(a_ref[...], w_ref[...].astype(BF16), preferred_element_type=F32)


def _out_proj(merged, w_out, layer):
    m, k = merged.shape
    n = w_out.shape[2]
    return pl.pallas_call(
        _out_body,
        grid=(m // TM_OUT, n // TN_OUT),
        in_specs=[pl.BlockSpec((TM_OUT, k), lambda i, j: (i, 0)),
                  pl.BlockSpec((None, k, TN_OUT), lambda i, j: (layer, 0, j))],
        out_specs=pl.BlockSpec((TM_OUT, TN_OUT), lambda i, j: (i, j)),
        out_shape=jax.ShapeDtypeStruct((m, n), F32),
        compiler_params=_params(("parallel", "parallel"), VMEM_MM),
        name="out_proj",
    )(merged, w_out)


def _mlstm_chunk(rows, m_prev, q_ref, k_ref, v_ref, o_ref, z_ref, g_ref, gb_ref, nw_ref, y_ref, state_ref):
    i_pre = g_ref[rows, :] + gb_ref[...]
    f_pre = pltpu.roll(i_pre, LANES - HEADS, 1)
    lf = -_softplus(-f_pre)
    b = lf
    d = 1
    while d < CHUNK:
        b = b + _shift_rows(b, d, 0.0)
        d *= 2
    beta = i_pre - b
    cm = beta
    d = 1
    while d < CHUNK:
        cm = jnp.maximum(cm, _shift_rows(cm, d, -jnp.inf))
        d *= 2
    inter = b + m_prev
    m_t = jnp.maximum(inter, b + cm)
    alpha = b - m_t
    w_inter = jnp.exp(inter - m_t)
    e_negm = jnp.exp(-m_t)
    b_last = b[CHUNK - 1:CHUNK]
    m_new = jnp.maximum(b_last + m_prev, b_last + cm[CHUNK - 1:CHUNK])
    decay = jnp.exp(b_last + m_prev - m_new)
    ws = jnp.exp(b_last + beta - m_new)
    beta_t = beta.T

    shape = (CHUNK, CHUNK)
    tril = _row_iota(shape) >= _lane_iota(shape)
    lane0 = _lane_iota(shape) == 0
    yield

    heads = range(HEADS)
    sls = [slice(h * HEAD_DIM, (h + 1) * HEAD_DIM) for h in heads]
    qs = [q_ref[rows, sl].astype(BF16) for sl in sls]
    k_ts = [(k_ref[rows, sl] * (HEAD_DIM ** -0.5)).T.astype(BF16) for sl in sls]
    qks = [jnp.dot(qs[h], k_ts[h], preferred_element_type=F32) for h in heads]
    q_states = [jnp.dot(qs[h], state_ref[h].astype(BF16), preferred_element_type=F32) for h in heads]
    scores = [qks[h] * jnp.exp(jnp.where(tril, alpha[:, h:h + 1] + beta_t[h:h + 1, :], -jnp.inf))
              for h in heads]
    pvs = [jnp.dot(scores[h].astype(BF16), v_ref[rows, sls[h]].astype(BF16), preferred_element_type=F32)
           for h in heads]
    yield
    wi_cols = [w_inter[:, h:h + 1] for h in heads]
    dens = [jnp.sum(scores[h], axis=-1, keepdims=True) + wi_cols[h] * q_states[h][:, HEAD_DIM:HEAD_DIM + 1]
            for h in heads]
    r_dens = [1.0 / jnp.maximum(jnp.abs(dens[h]), e_negm[:, h:h + 1]) for h in heads]
    hhs = [(pvs[h] + wi_cols[h] * q_states[h][:, :HEAD_DIM]) * r_dens[h] for h in heads]
    r_norms = [lax.rsqrt(jnp.mean(hh * hh, axis=-1, keepdims=True) + EPS) for hh in hhs]
    for h in heads:
        sl = sls[h]
        y = hhs[h] * r_norms[h] * nw_ref[:, sl]
        y = (y * _sigmoid(o_ref[rows, sl])) * _silu(z_ref[rows, sl])
        y_ref[rows, sl] = y.astype(y_ref.dtype)
    yield
    for h in heads:
        ws_col = ws[:, h:h + 1]
        upd = jnp.concatenate([(ws_col * v_ref[rows, sls[h]]).astype(BF16),
                               jnp.where(lane0, ws_col, 0.0).astype(BF16)], axis=1)
        state_ref[h] = decay[:, h:h + 1] * state_ref[h] + jnp.dot(k_ts[h], upd, preferred_element_type=F32)
    yield
    return m_new


def _mlstm_mix(q_ref, k_ref, v_ref, o_ref, z_ref, g_ref, gb_ref, nw_ref, y_ref, state_ref, m_ref):
    m = m_ref[...]
    for c in range(q_ref.shape[0] // CHUNK):
        rows = slice(c * CHUNK, (c + 1) * CHUNK)
        m = yield from _mlstm_chunk(rows, m, q_ref, k_ref, v_ref, o_ref, z_ref, g_ref, gb_ref, nw_ref, y_ref,
                                    state_ref)
    m_ref[...] = m


def _gmlp_mix(u_ref, v_ref, z_ref, lw_ref, lb_ref, ws_ref, bst_ref, y_ref):
    v = v_ref[...]
    mean = jnp.mean(v, axis=-1, keepdims=True)
    vc = v - mean
    rstd = lax.rsqrt(jnp.mean(vc * vc, axis=-1, keepdims=True) + EPS)
    shape = (CHUNK, CHUNK)
    tril = _row_iota(shape) >= _lane_iota(shape)
    yield
    for g in range(GROUPS):
        cols = slice(g * GROUP_W, (g + 1) * GROUP_W)
        w = jnp.where(tril, ws_ref[g], 0.0).astype(BF16)
        bias = bst_ref[:, g:g + 1]
        vn = ((v_ref[:, cols] - mean) * rstd * lw_ref[:, cols] + lb_ref[:, cols]).astype(BF16)
        for c in range(v.shape[0] // CHUNK):
            rows = slice(c * CHUNK, (c + 1) * CHUNK)
            sp = jnp.dot(w, vn[rows], preferred_element_type=F32) + bias
            y = u_ref[rows, cols] * sp * _silu(z_ref[rows, cols])
            y_ref[rows, cols] = y.astype(y_ref.dtype)
        yield


def _linear_scan(a, b, h0):
    rows, cols = a.shape
    groups = rows // SUBLANES
    a = a.reshape(groups, SUBLANES, cols)
    b = b.reshape(groups, SUBLANES, cols)
    sub = lax.broadcasted_iota(jnp.int32, a.shape, 1)
    d = 1
    while d < SUBLANES:
        keep = sub >= d
        b = b + a * jnp.where(keep, pltpu.roll(b, d, 1), 0.0)
        a = a * jnp.where(keep, pltpu.roll(a, d, 1), 1.0)
        d *= 2
    out = []
    for j in range(groups):
        hj = b[j] + a[j] * h0
        h0 = hj[SUBLANES - 1:]
        out.append(hj)
    return jnp.concatenate(out, axis=0)


def _rglru_mix(x_ref, z_ref, cw_ref, cb_ref, wg_ref, bg_ref, ap_ref, y_ref, prev_ref, carry_ref):
    step = pl.program_id(0)
    tm = x_ref.shape[0]
    sp = _softplus(-ap_ref[...])
    seq_start = (_row_iota((tm, LRU_BW)) == 0) & (step == 0)
    yield
    for n in range(LRU_BLOCKS):
        cols = slice(n * LRU_BW, (n + 1) * LRU_BW)
        x = x_ref[:, cols]
        xn = _causal_conv(x, prev_ref[:, cols], cw_ref[:, cols]) + cb_ref[:, cols]
        prev_ref[:, cols] = x[tm - SUBLANES:]
        gt = jnp.dot(xn.astype(BF16), wg_ref[n].astype(BF16), preferred_element_type=F32)
        r = _sigmoid(gt[:, :LRU_BW] + bg_ref[:, cols])
        ig = _sigmoid(gt[:, LRU_BW:] + bg_ref[:, BRANCH_W + n * LRU_BW:BRANCH_W + (n + 1) * LRU_BW])
        log_a = -LRU_C * r * sp[:, cols]
        a = jnp.exp(log_a)
        m2 = jnp.tanh(-log_a) * (a * a + 1.0)
        mult = jnp.where(m2 > 0.0, m2 * lax.rsqrt(m2), 0.0)
        mult = jnp.where(seq_start, 1.0, mult)
        hs = _linear_scan(a, mult * ig * xn, carry_ref[:, cols])
        carry_ref[:, cols] = hs[tm - 1:]
        y_ref[:, cols] = (hs * _silu(z_ref[:, cols])).astype(y_ref.dtype)
        yield


def _sconv_mix(b_ref, c_ref, x_ref, z_ref, w_ref, y_ref, prev_ref):
    tm = x_ref.shape[0]
    for n in range(BRANCH_W // LANES):
        cols = slice(n * LANES, (n + 1) * LANES)
        p = c_ref[:, cols] * x_ref[:, cols]
        conv = _causal_conv(p, prev_ref[:, cols], w_ref[:, cols])
        prev_ref[:, cols] = p[tm - SUBLANES:]
        y_ref[:, cols] = (b_ref[:, cols] * conv * _silu(z_ref[:, cols])).astype(y_ref.dtype)
        yield


def _alternate(*stage_generators):
    pending = list(stage_generators)
    while pending:
        for gen in list(pending):
            try:
                next(gen)
            except StopIteration:
                pending.remove(gen)


def _mixers_body(*refs):
    p = refs[:14]
    (g_ref, gb_ref, nw_ref, lw_ref, lb_ref, ws_ref, bst_ref, cw_ref, cb_ref, wg_ref, bg_ref, ap_ref, sw_ref,
     y_ref, state_ref, m_ref, lru_prev_ref, lru_carry_ref, conv_prev_ref) = refs[14:]

    @pl.when(pl.program_id(0) == 0)
    def _():
        for ref in (state_ref, m_ref, lru_prev_ref, lru_carry_ref, conv_prev_ref):
            ref[...] = jnp.zeros_like(ref)

    _alternate(_rglru_mix(*p[8:10], cw_ref, cb_ref, wg_ref, bg_ref, ap_ref, y_ref.at[2], lru_prev_ref,
                          lru_carry_ref),
               _mlstm_mix(*p[0:5], g_ref, gb_ref, nw_ref, y_ref.at[0], state_ref, m_ref),
               _sconv_mix(*p[10:14], sw_ref, y_ref.at[3], conv_prev_ref),
               _gmlp_mix(*p[5:8], lw_ref, lb_ref, ws_ref, bst_ref, y_ref.at[1]))


def _mixers(p_mix, gates, gate_bias, norm_w, ln_w, ln_b, w_s, b_s_t, conv_w, conv_b, w_gate, b_gate, a_param,
            sconv_w, layer):
    seq = p_mix.shape[0]
    tm = TM_MIX
    mix_specs = [pl.BlockSpec((tm, BRANCH_W), lambda i, c=c: (i, c)) for c in range(MIX_W // BRANCH_W)]
    vec = _layer_spec(layer, (1, BRANCH_W))
    return pl.pallas_call(
        _mixers_body,
        grid=(seq // tm,),
        in_specs=mix_specs + [
            _row_spec(tm, LANES), _full_spec((1, LANES)), vec,
            vec, vec, _layer_spec(layer, (GROUPS, CHUNK, CHUNK)), _full_spec((CHUNK, GROUPS)),
            _layer_spec(layer, conv_w.shape[1:]), vec, _layer_spec(layer, w_gate.shape[1:]),
            _layer_spec(layer, (1, 2 * BRANCH_W)), vec,
            _layer_spec(layer, sconv_w.shape[1:])],
        out_specs=pl.BlockSpec((N_BRANCH, tm, BRANCH_W), lambda i: (0, i, 0)),
        out_shape=jax.ShapeDtypeStruct((N_BRANCH, seq, BRANCH_W), BF16),
        scratch_shapes=[pltpu.VMEM((HEADS, HEAD_DIM, 2 * HEAD_DIM), F32), pltpu.VMEM((1, LANES), F32),
                        pltpu.VMEM((SUBLANES, BRANCH_W), F32), pltpu.VMEM((1, BRANCH_W), F32),
                        pltpu.VMEM((SUBLANES, BRANCH_W), F32)],
        compiler_params=_params(("arbitrary",), VMEM_MM),
        name="mixers",
    )(*([p_mix] * (MIX_W // BRANCH_W)), gates, gate_bias, norm_w, ln_w, ln_b, w_s, b_s_t, conv_w, conv_b,
      w_gate, b_gate, a_param, sconv_w)


def _merge_body(h_ref, y_ref, wg_ref, wb_ref, o_ref, pre_ref, acc_ref):
    s = pl.program_id(0)

    @pl.when(s == 0)
    def _():
        pre_ref[...] = jnp.zeros_like(pre_ref)
        acc_ref[...] = jnp.zeros_like(acc_ref)

    g_prev = lax.rem(jnp.maximum(s - 1, 0), N_BRANCH)
    gate = _sigmoid(pre_ref[lax.rem(s + 1, 2)])
    term = gate * jnp.dot(y_ref[...], wb_ref[...].astype(BF16), preferred_element_type=F32)
    acc = term + jnp.where(g_prev == 0, 0.0, acc_ref[...])
    acc_ref[...] = acc

    pre_ref[lax.rem(s, 2)] = _dot_nt(h_ref[...], wg_ref[...].astype(BF16))

    @pl.when(g_prev == N_BRANCH - 1)
    def _():
        o_ref[...] = acc.astype(o_ref.dtype)


def _merge(h, ys, w_in_t, w_branch, layer):
    seq = h.shape[0]
    n_tiles = D_MODEL // TN_MERGE
    per_row_tile = n_tiles * N_BRANCH
    n_items = (seq // TM_MERGE) * per_row_tile

    def item(s):
        return s // per_row_tile, lax.rem(s // N_BRANCH, n_tiles), lax.rem(s, N_BRANCH)

    def cur(s):
        return item(jnp.minimum(s, n_items - 1))

    def prev(s):
        return item(jnp.maximum(s - 1, 0))

    def gate_rows(s):
        _, j, g = cur(s)
        return _aligned_rows((MIX_W + N_GATE_COLS) // SUBLANES + (g * n_tiles + j) * (TN_MERGE // SUBLANES))

    return pl.pallas_call(
        _merge_body,
        grid=(n_items + 1,),
        in_specs=[pl.BlockSpec((TM_MERGE, D_MODEL), lambda s: (cur(s)[0], 0)),
                  pl.BlockSpec((None, TM_MERGE, BRANCH_W), lambda s: (prev(s)[2], prev(s)[0], 0)),
                  _wt_spec(layer, TN_MERGE, gate_rows),
                  pl.BlockSpec((None, None, BRANCH_W, TN_MERGE), lambda s: (layer, prev(s)[2], 0, prev(s)[1]))],
        out_specs=pl.BlockSpec((TM_MERGE, TN_MERGE), lambda s: (prev(s)[0], prev(s)[1])),
        out_shape=jax.ShapeDtypeStruct((seq, D_MODEL), BF16),
        scratch_shapes=[pltpu.VMEM((2, TM_MERGE, TN_MERGE), F32), pltpu.VMEM((TM_MERGE, TN_MERGE), F32)],
        compiler_params=_params(("arbitrary",), VMEM_MM),
        name="merge",
    )(h, ys, w_in_t, w_branch)


def kernel(x, pre_w, post_w, w_in, mlstm_gate_bias, mlstm_norm_w, gmlp_ln_w, gmlp_ln_b, gmlp_w_s, gmlp_b_s,
           rglru_conv_w, rglru_conv_b, rglru_w_gate, rglru_b_gate, rglru_a_param, sconv_w, w_branch, w_out):
    batch = x.shape[0]
    depth = w_in.shape[0]
    row = lambda p: p[:, None, :]
    pre_w, post_w = row(pre_w), row(post_w)
    norm_w, ln_w, ln_b = row(mlstm_norm_w), row(gmlp_ln_w), row(gmlp_ln_b)
    conv_b, b_gate, a_param = row(rglru_conv_b), row(rglru_b_gate), row(rglru_a_param)
    gate_bias = jnp.pad(mlstm_gate_bias, ((0, 0), (0, LANES - N_GATE_COLS)))
    b_s_t = jnp.swapaxes(gmlp_b_s, 1, 2)
    w_in_t = jnp.swapaxes(w_in, 1, 2)
    outs = []
    for bi in range(batch):
        xb = x[bi]
        h, gates = _prenorm(xb, pre_w, 0, w_in_t)
        for l in range(depth):
            p_mix = _proj(h, w_in_t, l)
            ys = _mixers(p_mix, gates, gate_bias[l][None], norm_w, ln_w, ln_b, gmlp_w_s, b_s_t[l],
                         rglru_conv_w, conv_b, rglru_w_gate, b_gate, a_param, sconv_w, l)
            merged = _merge(h, ys, w_in_t, w_branch, l)
            out = _out_proj(merged, w_out, l)
            if l + 1 < depth:
                xb, h, gates = _post_next(xb, out, post_w, pre_w, l, w_in_t)
            else:
                xb = _post(xb, out, post_w, l)
        outs.append(xb)
    return jnp.stack(outs, axis=0)
```
